```python
import math
import jax, jax.numpy as jnp
from jax import lax
import numpy as np

D_MODEL = 1024
BATCH = 8
SEQ = 8192
DEPTH = 1

CHUNK = 64
Q_BLOCK = 128
SB_HEADS = 8
SB_HEAD_DIM = 64
SB_WIDTH = SB_HEADS * SB_HEAD_DIM
DA_HEADS = 4
DA_HEAD_DIM = 64
DA_V_DIM = 2 * DA_HEAD_DIM
DA_QK_WIDTH = DA_HEADS * 2 * DA_HEAD_DIM
DA_V_WIDTH = DA_HEADS * DA_V_DIM
D_FF = 2816
N_SUB = 3
EPS = 1e-6
NEG = -1e30
IN_SIZES = (SB_WIDTH, SB_WIDTH, SB_WIDTH, DA_QK_WIDTH, DA_QK_WIDTH, DA_V_WIDTH, D_MODEL, D_MODEL)
IN_WIDTH = sum(IN_SIZES)
IN_SPLITS = tuple(int(v) for v in np.cumsum(IN_SIZES)[:-1])

kernel_name = "hybrid_stickbreak_diffattn_macaron_block"


def rms_norm(x, g):
    x32 = x.astype(jnp.float32)
    y = x32 * lax.rsqrt(jnp.mean(x32 * x32, axis=-1, keepdims=True) + EPS)
    return (y * g.astype(jnp.float32)).astype(x.dtype)


def swiglu(u, w_gate, w_up, w_down):
    return (jax.nn.silu(u @ w_gate) * (u @ w_up)) @ w_down


def to_blocks(t):
    b, h, s, d = t.shape
    return t.reshape(b, h, s // Q_BLOCK, Q_BLOCK, d).transpose(2, 0, 1, 3, 4)


def from_blocks(t):
    nb, b, h, qb, d = t.shape
    return t.transpose(1, 2, 0, 3, 4).reshape(b, h, nb * qb, d)


def stick_breaking_attention(q, k, v):
    s_len = q.shape[2]
    scale = 1.0 / math.sqrt(q.shape[-1])
    spos = jnp.arange(s_len)
    tpos = spos.reshape(s_len // Q_BLOCK, Q_BLOCK)

    def block(args):
        qi, ti = args
        z = jnp.einsum('bhqd,bhkd->bhqk', qi, k).astype(jnp.float32) * scale
        before = spos[None, :] < ti[:, None]
        log_fail = jnp.where(before, jax.nn.log_sigmoid(-z), 0.0)
        later = lax.cumsum(log_fail, axis=3, reverse=True) - log_fail
        w = jnp.where(before, jnp.exp(jax.nn.log_sigmoid(z) + later), 0.0)
        return jnp.einsum('bhqk,bhkd->bhqd', w.astype(v.dtype), v)

    return from_blocks(lax.map(block, (to_blocks(q), tpos)))


def differential_attention(q1, q2, k1, k2, v, lam):
    s_len = q1.shape[2]
    n_heads = q1.shape[1]
    scale = 1.0 / math.sqrt(q1.shape[-1])
    slopes = jnp.asarray(2.0 ** (-8.0 * (np.arange(n_heads) + 1) / n_heads), dtype=jnp.float32)
    spos = jnp.arange(s_len)
    tpos = spos.reshape(s_len // Q_BLOCK, Q_BLOCK)

    def block(args):
        qi1, qi2, ti = args
        allowed = (spos[None, :] // CHUNK) <= (ti[:, None] // CHUNK)
        dist = jnp.abs(ti[:, None] - spos[None, :]).astype(jnp.float32)
        bias = -slopes[:, None, None] * dist
        s1 = jnp.einsum('bhqd,bhkd->bhqk', qi1, k1).astype(jnp.float32) * scale + bias
        s2 = jnp.einsum('bhqd,bhkd->bhqk', qi2, k2).astype(jnp.float32) * scale + bias
        p = (jax.nn.softmax(jnp.where(allowed, s1, NEG), axis=-1)
             - lam * jax.nn.softmax(jnp.where(allowed, s2, NEG), axis=-1))
        return jnp.einsum('bhqk,bhkd->bhqd', p.astype(v.dtype), v)

    return from_blocks(lax.map(block, (to_blocks(q1), to_blocks(q2), tpos)))


def token_mixer(u, w_in, lq1, lk1, lq2, lk2, subln, w_branch_sb, w_branch_da, w_out, lambda_init):
    b, s, _ = u.shape
    proj = u @ w_in
    qa, ka, va, qd, kd, vd, ga, gd = jnp.split(proj, IN_SPLITS, axis=-1)

    def heads(t, h):
        return t.reshape(b, s, h, -1).transpose(0, 2, 1, 3)

    ya = stick_breaking_attention(heads(qa, SB_HEADS), heads(ka, SB_HEADS), heads(va, SB_HEADS))
    ya = ya.transpose(0, 2, 1, 3).reshape(b, s, SB_WIDTH)

    qd = qd.reshape(b, s, DA_HEADS, 2, DA_HEAD_DIM).transpose(0, 2, 3, 1, 4)
    kd = kd.reshape(b, s, DA_HEADS, 2, DA_HEAD_DIM).transpose(0, 2, 3, 1, 4)
    f32 = jnp.float32
    lam = (jnp.exp(jnp.sum(lq1.astype(f32) * lk1.astype(f32)))
           - jnp.exp(jnp.sum(lq2.astype(f32) * lk2.astype(f32))) + lambda_init)
    yd = differential_attention(qd[:, :, 0], qd[:, :, 1], kd[:, :, 0], kd[:, :, 1],
                                heads(vd, DA_HEADS), lam)
    yd = rms_norm(yd, subln) * (1.0 - lambda_init)
    yd = yd.transpose(0, 2, 1, 3).reshape(b, s, DA_V_WIDTH)

    merged = jax.nn.sigmoid(ga) * (ya @ w_branch_sb) + jax.nn.sigmoid(gd) * (yd @ w_branch_da)
    return merged @ w_out


def setup_inputs(seed: int = 0) -> dict:
    key = jax.random.key(seed)
    ks = jax.random.split(key, 24)
    f32 = jnp.float32

    def dense(k, shape, fan_in):
        return jax.random.normal(k, shape, f32) * fan_in ** -0.5

    L, D, F = DEPTH, D_MODEL, D_FF
    return {
        "x": jax.random.normal(ks[0], (BATCH, SEQ, D), f32),
        "c": jax.random.normal(ks[1], (BATCH, D), f32),
        "w_ada": dense(ks[2], (L, D, N_SUB * 3 * D), D),
        "b_ada": 0.01 * jax.random.normal(ks[3], (L, N_SUB * 3 * D), f32),
        "norm_pre": 1.0 + 0.05 * jax.random.normal(ks[4], (L, N_SUB, D), f32),
        "norm_post": 1.0 + 0.05 * jax.random.normal(ks[5], (L, N_SUB, D), f32),
        "ffn1_w_gate": dense(ks[6], (L, D, F), D),
        "ffn1_w_up": dense(ks[7], (L, D, F), D),
        "ffn1_w_down": dense(ks[8], (L, F, D), F),
        "w_in": dense(ks[9], (L, D, IN_WIDTH), D),
        "da_lambda_q1": 0.1 * jax.random.normal(ks[10], (L, DA_HEAD_DIM), f32),
        "da_lambda_k1": 0.1 * jax.random.normal(ks[11], (L, DA_HEAD_DIM), f32),
        "da_lambda_q2": 0.1 * jax.random.normal(ks[12], (L, DA_HEAD_DIM), f32),
        "da_lambda_k2": 0.1 * jax.random.normal(ks[13], (L, DA_HEAD_DIM), f32),
        "da_subln": 1.0 + 0.05 * jax.random.normal(ks[14], (L, DA_V_DIM), f32),
        "w_branch_sb": dense(ks[15], (L, SB_WIDTH, D), SB_WIDTH),
        "w_branch_da": dense(ks[16], (L, DA_V_WIDTH, D), DA_V_WIDTH),
        "w_out": dense(ks[17], (L, D, D), D),
        "ffn2_w_gate": dense(ks[18], (L, D, F), D),
        "ffn2_w_up": dense(ks[19], (L, D, F), D),
        "ffn2_w_down": dense(ks[20], (L, F, D), F),
    }


def reference(x, c, w_ada, b_ada, norm_pre, norm_post, ffn1_w_gate, ffn1_w_up, ffn1_w_down,
              w_in, da_lambda_q1, da_lambda_k1, da_lambda_q2, da_lambda_k2, da_subln,
              w_branch_sb, w_branch_da, w_out, ffn2_w_gate, ffn2_w_up, ffn2_w_down):
    b = x.shape[0]
    h = x
    for l in range(DEPTH):
        lambda_init = 0.8 - 0.6 * math.exp(-0.3 * l)
        mod = (jax.nn.silu(c) @ w_ada[l] + b_ada[l]).reshape(b, N_SUB, 3, D_MODEL)

        def sublayer(h, i, fn, resid_w):
            shift = mod[:, i, 0, None, :]
            scale = mod[:, i, 1, None, :]
            gate = mod[:, i, 2, None, :]
            u = rms_norm(h, norm_pre[l, i]) * (1.0 + scale) + shift
            return h + resid_w * gate * rms_norm(fn(u), norm_post[l, i])

        h = sublayer(h, 0, lambda u: swiglu(u, ffn1_w_gate[l], ffn1_w_up[l], ffn1_w_down[l]), 0.5)
        h = sublayer(h, 1, lambda u: token_mixer(u, w_in[l], da_lambda_q1[l], da_lambda_k1[l],
                                                 da_lambda_q2[l], da_lambda_k2[l], da_subln[l],
                                                 w_branch_sb[l], w_branch_da[l], w_out[l],
                                                 lambda_init), 1.0)
        h = sublayer(h, 2, lambda u: swiglu(u, ffn2_w_gate[l], ffn2_w_up[l], ffn2_w_down[l]), 0.5)
    return h
```

```python
import functools
import math

import jax
import jax.numpy as jnp
import numpy as np
from jax import lax
from jax.experimental import pallas as pl
from jax.experimental.pallas import tpu as pltpu

EPS = 1e-6
NEG = -1e30
N_SUB = 3
CHUNK_LOG2 = 6
SB_HEADS = 8
SB_HEAD_DIM = 64
DA_HEADS = 4
DA_HEAD_DIM = 64
DA_V_DIM = 2 * DA_HEAD_DIM
SUBLANES = 8
VMEM_LIMIT = 56 * 1024 * 1024

F32 = jnp.float32
BF16 = jnp.bfloat16


def _rms(x, g):
    return x * lax.rsqrt(jnp.mean(x * x, axis=-1, keepdims=True) + EPS) * g


def _dot(a, b):
    return jnp.dot(a, b, preferred_element_type=F32)


def _mod_kernel(c_ref, w_ref, b_ref, o_ref):
    c = c_ref[...]
    a = c * jax.nn.sigmoid(c)
    o_ref[...] = jnp.dot(a, w_ref[...], preferred_element_type=F32,
                         precision=lax.Precision.HIGHEST) + b_ref[...]


def _mod_call(c, w, b):
    bsz, d = c.shape
    n = w.shape[1]
    tn = 1024 if n % 1024 == 0 else n
    return pl.pallas_call(
        _mod_kernel,
        grid=(n // tn,),
        in_specs=[pl.BlockSpec((bsz, d), lambda j: (0, 0)),
                  pl.BlockSpec((d, tn), lambda j: (0, j)),
                  pl.BlockSpec((1, tn), lambda j: (0, j))],
        out_specs=pl.BlockSpec((bsz, tn), lambda j: (0, j)),
        out_shape=jax.ShapeDtypeStruct((bsz, n), F32),
        name="adaln_mod",
    )(c, w, b.reshape(1, n))


def _swiglu_sublayer(h, mod, npre, npost, wg_ref, wu_ref, wd_ref, sub, resid_w, fc):
    shift = mod[3 * sub + 0:3 * sub + 1, :]
    scale = mod[3 * sub + 1:3 * sub + 2, :]
    gate = mod[3 * sub + 2:3 * sub + 3, :]
    u = (_rms(h, npre[sub:sub + 1, :]) * (1.0 + scale) + shift).astype(BF16)
    f = wg_ref.shape[1]
    acc = jnp.zeros(h.shape, F32)
    for c0 in range(0, f, fc):
        g = _dot(u, wg_ref[:, c0:c0 + fc])
        up = _dot(u, wu_ref[:, c0:c0 + fc])
        a = (g * jax.nn.sigmoid(g) * up).astype(BF16)
        acc = acc + _dot(a, wd_ref[c0:c0 + fc, :])
    return h + resid_w * gate * _rms(acc, npost[sub:sub + 1, :])


def _ffn_kernel(h_ref, mod_ref, npre_ref, npost_ref, wg_ref, wu_ref, wd_ref, o_ref, *, sub, resid_w, fc):
    o_ref[0] = _swiglu_sublayer(h_ref[0], mod_ref[0], npre_ref[...], npost_ref[...],
                                wg_ref, wu_ref, wd_ref, sub, resid_w, fc)


def _const_spec(shape):
    nd = len(shape)
    return pl.BlockSpec(shape, lambda *_: (0,) * nd, pipeline_mode=pl.Buffered(1))


def _ffn_chunk(f):
    for fc in (256, 128):
        if f % fc == 0:
            return fc
    return f


def _ffn_call(h, mod, npre, npost, wg, wu, wd, *, sub, resid_w, tm):
    bsz, s, d = h.shape
    f = wg.shape[1]
    kern = functools.partial(_ffn_kernel, sub=sub, resid_w=resid_w, fc=_ffn_chunk(f))
    return pl.pallas_call(
        kern,
        grid=(bsz, s // tm),
        in_specs=[pl.BlockSpec((1, tm, d), lambda b, i: (b, i, 0)),
                  pl.BlockSpec((1, 3 * N_SUB, d), lambda b, i: (b, 0, 0)),
                  _const_spec(npre.shape), _const_spec(npost.shape),
                  _const_spec(wg.shape), _const_spec(wu.shape), _const_spec(wd.shape)],
        out_specs=pl.BlockSpec((1, tm, d), lambda b, i: (b, i, 0)),
        out_shape=jax.ShapeDtypeStruct(h.shape, F32),
        compiler_params=pltpu.CompilerParams(dimension_semantics=("arbitrary", "arbitrary"),
                                             vmem_limit_bytes=VMEM_LIMIT),
        name="ffn1",
    )(h, mod, npre, npost, wg, wu, wd)


def _proj_kernel(h_ref, mod_ref, npre_ref, w_ref, feat_ref, gate_ref, *, sub, n_feat, q_cols, q_scale, nc):
    h = h_ref[0]
    mod = mod_ref[0]
    shift = mod[3 * sub + 0:3 * sub + 1, :]
    scale = mod[3 * sub + 1:3 * sub + 2, :]
    u = (_rms(h, npre_ref[sub:sub + 1, :]) * (1.0 + scale) + shift).astype(BF16)
    n_all = w_ref.shape[1]
    for c0 in range(0, n_all, nc):
        p = _dot(u, w_ref[:, c0:c0 + nc])
        if c0 < n_feat:
            if any(lo <= c0 < hi for lo, hi in q_cols):
                p = p * q_scale
            feat_ref[0, :, c0:c0 + nc] = p.astype(BF16)
        else:
            gate_ref[0, :, c0 - n_feat:c0 - n_feat + nc] = jax.nn.sigmoid(p).astype(BF16)


def _proj_call(h, mod, npre, w_in, *, n_feat, q_cols, q_scale, tm):
    bsz, s, d = h.shape
    n_all = w_in.shape[1]
    kern = functools.partial(_proj_kernel, sub=1, n_feat=n_feat, q_cols=q_cols, q_scale=q_scale, nc=512)
    return pl.pallas_call(
        kern,
        grid=(bsz, s // tm),
        in_specs=[pl.BlockSpec((1, tm, d), lambda b, i: (b, i, 0)),
                  pl.BlockSpec((1, 3 * N_SUB, d), lambda b, i: (b, 0, 0)),
                  _const_spec(npre.shape), _const_spec(w_in.shape)],
        out_specs=[pl.BlockSpec((1, tm, n_feat), lambda b, i: (b, i, 0)),
                   pl.BlockSpec((1, tm, n_all - n_feat), lambda b, i: (b, i, 0))],
        out_shape=[jax.ShapeDtypeStruct((bsz, s, n_feat), BF16),
                   jax.ShapeDtypeStruct((bsz, s, n_all - n_feat), BF16)],
        compiler_params=pltpu.CompilerParams(dimension_semantics=("arbitrary", "arbitrary"),
                                             vmem_limit_bytes=VMEM_LIMIT),
        name="mixer_in_proj",
    )(h, mod, npre, w_in)


def _sb_key_local(row0, nrows, ncols, kb):
    rho = row0 + lax.broadcasted_iota(jnp.int32, (nrows, ncols), 0)
    return (rho & (SUBLANES - 1)) * (kb // SUBLANES) + (rho >> 3)


def _sb_kernel(qT_ref, k_ref, vT_ref, o_ref, z_scr, w_scr, acc_scr, *, kb, tq):
    qi = pl.program_id(2)
    nslab = kb // SUBLANES
    qT = qT_ref[0, 0]
    acc_scr[...] = jnp.zeros_like(acc_scr)

    def block(j, carry, masked):
        z_scr[...] = _dot(k_ref[0, 0, j], qT)

        def slab(i, run):
            v = nslab - 1 - i
            rows = pl.ds(pl.multiple_of(v * SUBLANES, SUBLANES), SUBLANES)
            z = z_scr[rows, :]
            sp = jnp.maximum(z, 0.0) + jnp.log(1.0 + jnp.exp(-jnp.abs(z)))
            if masked:
                before = _sb_key_local(v * SUBLANES, SUBLANES, tq, kb) < lax.broadcasted_iota(
                    jnp.int32, (SUBLANES, tq), 1)
                sp = jnp.where(before, sp, 0.0)
            run = run + sp
            z_scr[rows, :] = z - run
            return run

        tot = lax.fori_loop(0, nslab, slab, jnp.zeros((SUBLANES, tq), F32), unroll=4)
        s = carry
        offs = [None] * SUBLANES
        for r in range(SUBLANES - 1, -1, -1):
            offs[r] = s
            s = s + tot[r:r + 1, :]
        off = jnp.concatenate(offs, axis=0)
        off2 = jnp.concatenate([off, off], axis=0)

        def expo(i, _):
            rows = pl.ds(pl.multiple_of(i * 2 * SUBLANES, 2 * SUBLANES), 2 * SUBLANES)
            w = jnp.exp(z_scr[rows, :] - off2)
            if masked:
                before = _sb_key_local(i * 2 * SUBLANES, 2 * SUBLANES, tq, kb) < lax.broadcasted_iota(
                    jnp.int32, (2 * SUBLANES, tq), 1)
                w = jnp.where(before, w, 0.0)
            w_scr[rows, :] = w.astype(BF16)
            return 0

        lax.fori_loop(0, nslab // 2, expo, 0, unroll=4)
        acc_scr[...] += _dot(vT_ref[0, 0, j], w_scr[...])
        return s

    carry = block(qi, jnp.zeros((1, tq), F32), True)
    lax.fori_loop(0, qi, lambda i, cr: block(qi - 1 - i, cr, False), carry)
    o_ref[0, 0] = acc_scr[...].astype(o_ref.dtype)


def _sb_call(qT, kP, vT, *, kb, tq):
    bsz, nh, dh, s = qT.shape
    nkb = s // kb
    kern = functools.partial(_sb_kernel, kb=kb, tq=tq)
    return pl.pallas_call(
        kern,
        grid=(bsz, nh, s // tq),
        in_specs=[pl.BlockSpec((1, 1, dh, tq), lambda b, h, i: (b, h, 0, i)),
                  pl.BlockSpec((1, 1, nkb, kb, dh), lambda b, h, i: (b, h, 0, 0, 0)),
                  pl.BlockSpec((1, 1, nkb, dh, kb), lambda b, h, i: (b, h, 0, 0, 0))],
        out_specs=pl.BlockSpec((1, 1, dh, tq), lambda b, h, i: (b, h, 0, i)),
        out_shape=jax.ShapeDtypeStruct((bsz, nh, dh, s), BF16),
        scratch_shapes=[pltpu.VMEM((kb, tq), F32), pltpu.VMEM((kb, tq), BF16), pltpu.VMEM((dh, tq), F32)],
        compiler_params=pltpu.CompilerParams(dimension_semantics=("arbitrary",) * 3,
                                             vmem_limit_bytes=VMEM_LIMIT),
        name="stickbreak_attn",
    )(qT, kP, vT)


def _da_kernel(slope_ref, lq1_ref, lk1_ref, lq2_ref, lk2_ref, subln_ref, qT_ref, k_ref, vT_ref, o_ref,
               m_scr, l_scr, acc_scr, *, kb, tq, lambda_init):
    h = pl.program_id(1)
    qi = pl.program_id(2)
    slope = slope_ref[h]
    dh = qT_ref.shape[2] // 2
    qT = qT_ref[0, 0]
    qrow = lax.broadcasted_iota(jnp.int32, qT.shape, 0)
    q_pad = (jnp.where(qrow < dh, qT, jnp.zeros_like(qT)), jnp.where(qrow >= dh, qT, jnp.zeros_like(qT)))
    m_scr[...] = jnp.full_like(m_scr, NEG)
    l_scr[...] = jnp.zeros_like(l_scr)
    acc_scr[...] = jnp.zeros_like(acc_scr)
    base = slope * lax.broadcasted_iota(jnp.int32, (kb, tq), 0).astype(F32)

    def block(j, diagonal):
        kblk = k_ref[0, pl.ds(pl.multiple_of(j * kb, kb), kb), :]
        vblk = vT_ref[0, 0, j]
        if diagonal:
            kl = lax.broadcasted_iota(jnp.int32, (kb, tq), 0)
            tl = lax.broadcasted_iota(jnp.int32, (kb, tq), 1)
            allowed = (kl >> CHUNK_LOG2) <= (tl >> CHUNK_LOG2)
            bias = slope * (tl - jnp.abs(tl - kl)).astype(F32)
            cj = 0.0
        else:
            bias = base
            cj = slope * ((j - qi) * kb).astype(F32)
        for mp in range(2):
            z = _dot(kblk, q_pad[mp]) + bias
            if diagonal:
                z = jnp.where(allowed, z, NEG)
            m_old = m_scr[mp]
            m_new = jnp.maximum(m_old, jnp.max(z, axis=0, keepdims=True) + cj)
            alpha = jnp.exp(m_old - m_new)
            p = jnp.exp(z - (m_new - cj))
            l_scr[mp] = alpha * l_scr[mp] + jnp.sum(p, axis=0, keepdims=True)
            acc_scr[mp] = alpha * acc_scr[mp] + _dot(vblk, p.astype(BF16))
            m_scr[mp] = m_new

    block(qi, True)

    def body(i, _):
        block(qi - 1 - i, False)
        return 0

    lax.fori_loop(0, qi, body, 0)

    lam = (jnp.exp(jnp.sum(lq1_ref[...] * lk1_ref[...], axis=1, keepdims=True))
           - jnp.exp(jnp.sum(lq2_ref[...] * lk2_ref[...], axis=1, keepdims=True)) + lambda_init)
    y = acc_scr[0] / l_scr[0] - lam * (acc_scr[1] / l_scr[1])
    y = y * lax.rsqrt(jnp.mean(y * y, axis=0, keepdims=True) + EPS)
    o_ref[0, 0] = (y * subln_ref[...] * (1.0 - lambda_init)).astype(o_ref.dtype)


def _da_call(slopes, lq1, lk1, lq2, lk2, subln, qT, k_all, vT, *, k_col0, kb, tq, lambda_init):
    bsz, nh, dq2, s = qT.shape
    dv = vT.shape[3]
    nkb = s // kb
    kern = functools.partial(_da_kernel, kb=kb, tq=tq, lambda_init=lambda_init)
    kblk0 = k_col0 // dq2
    vec = lambda a: pl.BlockSpec(a.shape, lambda b, h, i: (0, 0))
    return pl.pallas_call(
        kern,
        grid=(bsz, nh, s // tq),
        in_specs=[pl.BlockSpec(memory_space=pltpu.SMEM),
                  vec(lq1), vec(lk1), vec(lq2), vec(lk2), vec(subln),
                  pl.BlockSpec((1, 1, dq2, tq), lambda b, h, i: (b, h, 0, i)),
                  pl.BlockSpec((1, s, dq2), lambda b, h, i: (b, 0, kblk0 + h)),
                  pl.BlockSpec((1, 1, nkb, dv, kb), lambda b, h, i: (b, h, 0, 0, 0))],
        out_specs=pl.BlockSpec((1, 1, dv, tq), lambda b, h, i: (b, h, 0, i)),
        out_shape=jax.ShapeDtypeStruct((bsz, nh, dv, s), BF16),
        scratch_shapes=[pltpu.VMEM((2, 1, tq), F32), pltpu.VMEM((2, 1, tq), F32), pltpu.VMEM((2, dv, tq), F32)],
        compiler_params=pltpu.CompilerParams(dimension_semantics=("arbitrary",) * 3,
                                             vmem_limit_bytes=VMEM_LIMIT),
        name="diff_attn",
    )(slopes, lq1, lk1, lq2, lk2, subln, qT, k_all, vT)


def _merge_ffn_kernel(h_ref, mod_ref, npre_ref, npost_ref, ya_ref, yd_ref, gate_ref, wsb_ref, wda_ref, wout_ref,
                      wg_ref, wu_ref, wd_ref, o_ref, *, fc):
    h = h_ref[0]
    mod = mod_ref[0]
    d = h.shape[1]
    a = _dot(ya_ref[0], wsb_ref[...])
    b = _dot(yd_ref[0], wda_ref[...])
    merged = gate_ref[0, :, 0:d].astype(F32) * a + gate_ref[0, :, d:2 * d].astype(F32) * b
    m = _dot(merged.astype(BF16), wout_ref[...])
    h2 = h + mod[5:6, :] * _rms(m, npost_ref[1:2, :])
    o_ref[0] = _swiglu_sublayer(h2, mod, npre_ref[...], npost_ref[...], wg_ref, wu_ref, wd_ref, 2, 0.5, fc)


def _merge_ffn_call(h, mod, npre, npost, ya, yd, gates, wsb, wda, wout, wg, wu, wd, *, tm):
    bsz, s, d = h.shape
    kern = functools.partial(_merge_ffn_kernel, fc=_ffn_chunk(wg.shape[1]))
    tok = lambda w: pl.BlockSpec((1, tm, w), lambda b, i: (b, i, 0))
    return pl.pallas_call(
        kern,
        grid=(bsz, s // tm),
        in_specs=[tok(d), pl.BlockSpec((1, 3 * N_SUB, d), lambda b, i: (b, 0, 0)),
                  _const_spec(npre.shape), _const_spec(npost.shape),
                  tok(ya.shape[2]), tok(yd.shape[2]), tok(gates.shape[2]),
                  _const_spec(wsb.shape), _const_spec(wda.shape), _const_spec(wout.shape),
                  _const_spec(wg.shape), _const_spec(wu.shape), _const_spec(wd.shape)],
        out_specs=tok(d),
        out_shape=jax.ShapeDtypeStruct(h.shape, F32),
        compiler_params=pltpu.CompilerParams(dimension_semantics=("arbitrary", "arbitrary"),
                                             vmem_limit_bytes=VMEM_LIMIT),
        name="merge_out_ffn2",
    )(h, mod, npre, npost, ya, yd, gates, wsb, wda, wout, wg, wu, wd)


def _layer(h, c, w_ada, b_ada, npre, npost, f1g, f1u, f1d, w_in, lq1, lk1, lq2, lk2, subln,
           w_sb, w_da, w_out, f2g, f2u, f2d, lambda_init):
    bsz, s, d = h.shape
    tm = min(512, s)
    kb = tq = min(512, s)
    nkb = s // kb
    sbw = SB_HEADS * SB_HEAD_DIM
    daw = DA_HEADS * 2 * DA_HEAD_DIM
    dvw = DA_HEADS * DA_V_DIM
    n_feat = 3 * sbw + 2 * daw + dvw
    bf = lambda w: w.astype(BF16)

    mod = _mod_call(c, w_ada, b_ada).reshape(bsz, 3 * N_SUB, d)
    h1 = _ffn_call(h, mod, npre, npost, bf(f1g), bf(f1u), bf(f1d), sub=0, resid_w=0.5, tm=tm)
    q_cols = ((0, sbw), (3 * sbw, 3 * sbw + daw))
    feat, gates = _proj_call(h1, mod, npre, bf(w_in), n_feat=n_feat, q_cols=q_cols,
                             q_scale=1.0 / math.sqrt(SB_HEAD_DIM), tm=tm)

    qa = feat[..., 0:sbw].reshape(bsz, s, SB_HEADS, SB_HEAD_DIM).transpose(0, 2, 3, 1)
    ka = feat[..., sbw:2 * sbw].reshape(bsz, nkb, SUBLANES, kb // SUBLANES, SB_HEADS, SB_HEAD_DIM)
    ka = ka.transpose(0, 4, 1, 3, 2, 5).reshape(bsz, SB_HEADS, nkb, kb, SB_HEAD_DIM)
    va = feat[..., 2 * sbw:3 * sbw].reshape(bsz, nkb, SUBLANES, kb // SUBLANES, SB_HEADS, SB_HEAD_DIM)
    va = va.transpose(0, 4, 1, 5, 3, 2).reshape(bsz, SB_HEADS, nkb, SB_HEAD_DIM, kb)
    qd = feat[..., 3 * sbw:3 * sbw + daw].reshape(bsz, s, DA_HEADS, 2 * DA_HEAD_DIM).transpose(0, 2, 3, 1)
    vd = feat[..., 3 * sbw + 2 * daw:].reshape(bsz, nkb, kb, DA_HEADS, DA_V_DIM).transpose(0, 3, 1, 4, 2)

    yaT = _sb_call(qa, ka, va, kb=kb, tq=tq)
    slopes = jnp.asarray(2.0 ** (-8.0 * (np.arange(DA_HEADS) + 1) / DA_HEADS), dtype=F32)
    row = lambda v: v.reshape(1, -1).astype(F32)
    ydT = _da_call(slopes, row(lq1), row(lk1), row(lq2), row(lk2), subln.reshape(-1, 1).astype(F32),
                   qd, feat, vd, k_col0=3 * sbw + daw, kb=kb, tq=tq, lambda_init=lambda_init)
    ya = yaT.transpose(0, 3, 1, 2).reshape(bsz, s, sbw)
    yd = ydT.transpose(0, 3, 1, 2).reshape(bsz, s, dvw)
    return _merge_ffn_call(h1, mod, npre, npost, ya, yd, gates, bf(w_sb), bf(w_da), bf(w_out),
                           bf(f2g), bf(f2u), bf(f2d), tm=tm)


def kernel(x, c, w_ada, b_ada, norm_pre, norm_post, ffn1_w_gate, ffn1_w_up, ffn1_w_down, w_in, da_lambda_q1, da_lambda_k1, da_lambda_q2, da_lambda_k2, da_subln, w_branch_sb, w_branch_da, w_out, ffn2_w_gate, ffn2_w_up, ffn2_w_down):
    h = x
    for l in range(w_ada.shape[0]):
        lambda_init = 0.8 - 0.6 * math.exp(-0.3 * l)
        h = _layer(h, c, w_ada[l], b_ada[l], norm_pre[l], norm_post[l],
                   ffn1_w_gate[l], ffn1_w_up[l], ffn1_w_down[l], w_in[l],
                   da_lambda_q1[l], da_lambda_k1[l], da_lambda_q2[l], da_lambda_k2[l], da_subln[l],
                   w_branch_sb[l], w_branch_da[l], w_out[l],
                   ffn2_w_gate[l], ffn2_w_up[l], ffn2_w_down[l], lambda_init)
    return h
```

```python
import functools
import math

import jax
import jax.numpy as jnp
import numpy as np
from jax import lax
from jax.experimental import pallas as pl
from jax.experimental.pallas import tpu as pltpu

EPS = 1e-6
NEG = -1e30
N_SUB = 3
CHUNK_LOG2 = 6
SB_HEADS = 8
SB_HEAD_DIM = 64
DA_HEADS = 4
DA_HEAD_DIM = 64
DA_V_DIM = 2 * DA_HEAD_DIM
SUBLANES = 8
MXU_DIM = 256
LANES = 128
SB_QK_ROWS = 128
DA_QK_ROWS = 128
LOG2E = 1.4426950408889634
VMEM_LIMIT = 56 * 1024 * 1024

F32 = jnp.float32
BF16 = jnp.bfloat16


def _rms(x, g):
    return x * lax.rsqrt(jnp.mean(x * x, axis=-1, keepdims=True) + EPS) * g


def _dot(a, b):
    return jnp.dot(a, b, preferred_element_type=F32)


def _mod_kernel(c_ref, w_ref, b_ref, o_ref):
    c = c_ref[...]
    a = c * jax.nn.sigmoid(c)
    o_ref[...] = jnp.dot(a, w_ref[...], preferred_element_type=F32,
                         precision=lax.Precision.HIGHEST) + b_ref[...]


def _mod_call(c, w, b):
    bsz, d = c.shape
    n = w.shape[1]
    tn = 1024 if n % 1024 == 0 else n
    return pl.pallas_call(
        _mod_kernel,
        grid=(n // tn,),
        in_specs=[pl.BlockSpec((bsz, d), lambda j: (0, 0)),
                  pl.BlockSpec((d, tn), lambda j: (0, j)),
                  pl.BlockSpec((1, tn), lambda j: (0, j))],
        out_specs=pl.BlockSpec((bsz, tn), lambda j: (0, j)),
        out_shape=jax.ShapeDtypeStruct((bsz, n), F32),
        name="adaln_mod",
    )(c, w, b.reshape(1, n))


def _swiglu_sublayer(h, mod, npre, npost, wg_ref, wu_ref, wd_ref, sub, resid_w, fc):
    shift = mod[3 * sub + 0:3 * sub + 1, :]
    scale = mod[3 * sub + 1:3 * sub + 2, :]
    gate = mod[3 * sub + 2:3 * sub + 3, :]
    u = (_rms(h, npre[sub:sub + 1, :]) * (1.0 + scale) + shift).astype(BF16)
    f = wg_ref.shape[1]
    acc = jnp.zeros(h.shape, F32)
    for c0 in range(0, f, fc):
        g = _dot(u, wg_ref[:, c0:c0 + fc])
        up = _dot(u, wu_ref[:, c0:c0 + fc])
        a = (g * jax.nn.sigmoid(g) * up).astype(BF16)
        acc = acc + _dot(a, wd_ref[c0:c0 + fc, :])
    return h + resid_w * gate * _rms(acc, npost[sub:sub + 1, :])


def _ffn_kernel(h_ref, mod_ref, npre_ref, npost_ref, wg_ref, wu_ref, wd_ref, o_ref, *, sub, resid_w, fc):
    o_ref[0] = _swiglu_sublayer(h_ref[0], mod_ref[0], npre_ref[...], npost_ref[...],
                                wg_ref, wu_ref, wd_ref, sub, resid_w, fc)


def _const_spec(shape):
    nd = len(shape)
    return pl.BlockSpec(shape, lambda *_: (0,) * nd, pipeline_mode=pl.Buffered(1))


def _ffn_chunk(f):
    for fc in (256, 128):
        if f % fc == 0:
            return fc
    return f


def _ffn_call(h, mod, npre, npost, wg, wu, wd, *, sub, resid_w, tm):
    bsz, s, d = h.shape
    f = wg.shape[1]
    kern = functools.partial(_ffn_kernel, sub=sub, resid_w=resid_w, fc=_ffn_chunk(f))
    return pl.pallas_call(
        kern,
        grid=(bsz, s // tm),
        in_specs=[pl.BlockSpec((1, tm, d), lambda b, i: (b, i, 0)),
                  pl.BlockSpec((1, 3 * N_SUB, d), lambda b, i: (b, 0, 0)),
                  _const_spec(npre.shape), _const_spec(npost.shape),
                  _const_spec(wg.shape), _const_spec(wu.shape), _const_spec(wd.shape)],
        out_specs=pl.BlockSpec((1, tm, d), lambda b, i: (b, i, 0)),
        out_shape=jax.ShapeDtypeStruct(h.shape, F32),
        compiler_params=pltpu.CompilerParams(dimension_semantics=("arbitrary", "arbitrary"),
                                             vmem_limit_bytes=VMEM_LIMIT),
        name="ffn1",
    )(h, mod, npre, npost, wg, wu, wd)


def _proj_kernel(h_ref, mod_ref, npre_ref, w_ref, feat_ref, gate_ref, *, sub, n_feat, q_cols, q_scale, nc):
    h = h_ref[0]
    mod = mod_ref[0]
    shift = mod[3 * sub + 0:3 * sub + 1, :]
    scale = mod[3 * sub + 1:3 * sub + 2, :]
    u = (_rms(h, npre_ref[sub:sub + 1, :]) * (1.0 + scale) + shift).astype(BF16)
    n_all = w_ref.shape[1]
    for c0 in range(0, n_all, nc):
        p = _dot(u, w_ref[:, c0:c0 + nc])
        if c0 < n_feat:
            if any(lo <= c0 < hi for lo, hi in q_cols):
                p = p * q_scale
            feat_ref[0, :, c0:c0 + nc] = p.astype(BF16)
        else:
            gate_ref[0, :, c0 - n_feat:c0 - n_feat + nc] = jax.nn.sigmoid(p).astype(BF16)


def _proj_call(h, mod, npre, w_in, *, n_feat, q_cols, q_scale, tm):
    bsz, s, d = h.shape
    n_all = w_in.shape[1]
    kern = functools.partial(_proj_kernel, sub=1, n_feat=n_feat, q_cols=q_cols, q_scale=q_scale, nc=512)
    return pl.pallas_call(
        kern,
        grid=(bsz, s // tm),
        in_specs=[pl.BlockSpec((1, tm, d), lambda b, i: (b, i, 0)),
                  pl.BlockSpec((1, 3 * N_SUB, d), lambda b, i: (b, 0, 0)),
                  _const_spec(npre.shape), _const_spec(w_in.shape)],
        out_specs=[pl.BlockSpec((1, tm, n_feat), lambda b, i: (b, i, 0)),
                   pl.BlockSpec((1, tm, n_all - n_feat), lambda b, i: (b, i, 0))],
        out_shape=[jax.ShapeDtypeStruct((bsz, s, n_feat), BF16),
                   jax.ShapeDtypeStruct((bsz, s, n_all - n_feat), BF16)],
        compiler_params=pltpu.CompilerParams(dimension_semantics=("arbitrary", "arbitrary"),
                                             vmem_limit_bytes=VMEM_LIMIT),
        name="mixer_in_proj",
    )(h, mod, npre, w_in)


def _neg_abs(z):
    bits = lax.bitcast_convert_type(z, jnp.uint32) | jnp.uint32(0x80000000)
    return lax.bitcast_convert_type(bits, F32)


def _sb_kernel(qT_ref, k_ref, vT_ref, o_ref, z_scr, y_scr, w_scr, acc_scr, *, kb, tq):
    qi = pl.program_id(2)
    seg = kb // SUBLANES
    qk_rows = min(SB_QK_ROWS, kb)
    pv_rows = min(MXU_DIM, kb)
    qT = qT_ref[0, 0]
    acc_scr[...] = jnp.zeros_like(acc_scr)

    def scores(j):
        for c0 in range(kb - qk_rows, -1, -qk_rows):
            z_scr[c0:c0 + qk_rows, :] = _dot(k_ref[0, 0, j, c0:c0 + qk_rows, :], qT)

    def values(j):
        for c0 in range(0, kb, pv_rows):
            acc_scr[...] += _dot(vT_ref[0, 0, j, :, c0:c0 + pv_rows], w_scr[c0:c0 + pv_rows, :])

    def block(j, carry, first):
        if first:
            q_local = lax.broadcasted_iota(jnp.int32, (SUBLANES, tq), 1)
            seg_start = lax.broadcasted_iota(jnp.int32, (SUBLANES, tq), 0) * seg
        else:
            values(j + 1)
        run = jnp.zeros((SUBLANES, tq), F32)
        for r0 in range(kb - SUBLANES, -1, -SUBLANES):
            z = z_scr[r0:r0 + SUBLANES, :]
            sp = jnp.maximum(z, 0.0) + LOG2E * jnp.log(1.0 + jnp.exp2(_neg_abs(z)))
            if first:
                sp = jnp.where(seg_start + r0 // SUBLANES < q_local, sp, 0.0)
            run = run + sp
            y_scr[r0:r0 + SUBLANES, :] = z - run
        s = carry
        offs = [None] * SUBLANES
        for r in range(SUBLANES - 1, -1, -1):
            offs[r] = s
            s = s + run[r:r + 1, :]
        off = jnp.concatenate(offs, axis=0)
        scores(jnp.maximum(j - 1, 0))
        off2 = jnp.concatenate([off, off], axis=0)
        for r0 in range(0, kb, 2 * SUBLANES):
            w = jnp.exp2(y_scr[r0:r0 + 2 * SUBLANES, :] - off2)
            if first:
                v = r0 // SUBLANES
                before = jnp.concatenate([seg_start + v < q_local, seg_start + (v + 1) < q_local], axis=0)
                w = jnp.where(before, w, 0.0)
            w_scr[r0:r0 + 2 * SUBLANES, :] = w.astype(BF16)
        return s

    scores(qi)
    carry = block(qi, jnp.zeros((1, tq), F32), True)
    lax.fori_loop(0, qi, lambda i, cr: block(qi - 1 - i, cr, False), carry)
    values(0)
    o_ref[0, 0] = acc_scr[...].astype(o_ref.dtype)


def _sb_call(qT, kP, vT, *, kb, tq):
    bsz, nh, dh, s = qT.shape
    nkb = s // kb
    kern = functools.partial(_sb_kernel, kb=kb, tq=tq)
    return pl.pallas_call(
        kern,
        grid=(bsz, nh, s // tq),
        in_specs=[pl.BlockSpec((1, 1, dh, tq), lambda b, h, i: (b, h, 0, i)),
                  pl.BlockSpec((1, 1, nkb, kb, dh), lambda b, h, i: (b, h, 0, 0, 0)),
                  pl.BlockSpec((1, 1, nkb, dh, kb), lambda b, h, i: (b, h, 0, 0, 0))],
        out_specs=pl.BlockSpec((1, 1, dh, tq), lambda b, h, i: (b, h, 0, i)),
        out_shape=jax.ShapeDtypeStruct((bsz, nh, dh, s), BF16),
        scratch_shapes=[pltpu.VMEM((kb, tq), F32), pltpu.VMEM((kb, tq), F32), pltpu.VMEM((kb, tq), BF16),
                        pltpu.VMEM((dh, tq), F32)],
        compiler_params=pltpu.CompilerParams(dimension_semantics=("arbitrary",) * 3,
                                             vmem_limit_bytes=VMEM_LIMIT),
        name="stickbreak_attn",
    )(qT, kP, vT)


def _da_kernel(slope_ref, lq1_ref, lk1_ref, lq2_ref, lk2_ref, subln_ref, qT_ref, k_ref, vT_ref, o_ref,
               s_scr, p_scr, base_scr, mx_scr, alpha_scr, m_scr, l_scr, acc_scr, *, kb, tq, lambda_init):
    h = pl.program_id(1)
    qi = pl.program_id(2)
    slope = slope_ref[h]
    dh = qT_ref.shape[2] // 2
    qk_rows = min(DA_QK_ROWS, kb)
    pv_rows = min(MXU_DIM, kb)
    lane_blocks = tq // LANES
    qT = qT_ref[0, 0]
    qrow = lax.broadcasted_iota(jnp.int32, qT.shape, 0)
    q_pad = (jnp.where(qrow < dh, qT, jnp.zeros_like(qT)), jnp.where(qrow >= dh, qT, jnp.zeros_like(qT)))
    m_scr[...] = jnp.full_like(m_scr, NEG)
    l_scr[...] = jnp.zeros_like(l_scr)
    acc_scr[...] = jnp.zeros_like(acc_scr)
    base_scr[...] = slope * lax.broadcasted_iota(jnp.int32, (kb, LANES), 0).astype(F32)

    def scores(j, mp, diagonal):
        mx = jnp.full((SUBLANES, tq), NEG, F32)
        for c0 in range(0, kb, qk_rows):
            rows = pl.ds(pl.multiple_of(j * kb + c0, qk_rows), qk_rows)
            zc = _dot(k_ref[0, rows, :], q_pad[mp])
            if diagonal:
                kl = c0 + lax.broadcasted_iota(jnp.int32, (qk_rows, tq), 0)
                tl = lax.broadcasted_iota(jnp.int32, (qk_rows, tq), 1)
                zc = zc + slope * (tl - jnp.abs(tl - kl)).astype(F32)
                zc = jnp.where((kl >> CHUNK_LOG2) <= (tl >> CHUNK_LOG2), zc, NEG)
            else:
                zc = zc + jnp.concatenate([base_scr[c0:c0 + qk_rows, :]] * lane_blocks, axis=1)
            s_scr[mp, c0:c0 + qk_rows, :] = zc
            for r0 in range(0, qk_rows, SUBLANES):
                mx = jnp.maximum(mx, zc[r0:r0 + SUBLANES, :])
        mx_scr[mp] = mx

    def values(j, mp):
        pv = None
        for c0 in range(0, kb, pv_rows):
            d = _dot(vT_ref[0, 0, j, :, c0:c0 + pv_rows], p_scr[mp, c0:c0 + pv_rows, :])
            pv = d if pv is None else pv + d
        acc_scr[mp] = alpha_scr[mp] * acc_scr[mp] + pv

    def block(j, first):
        cj = 0.0 if first else slope * ((j - qi) * kb).astype(F32)
        for mp in range(2):
            if not first:
                values(j + 1, mp)
            m_old = m_scr[mp]
            m_new = jnp.maximum(m_old, jnp.max(mx_scr[mp], axis=0, keepdims=True) + cj)
            alpha = jnp.exp2(m_old - m_new)
            shift = jnp.broadcast_to(m_new - cj, (2 * SUBLANES, tq))
            lsum = jnp.zeros((SUBLANES, tq), F32)
            for r0 in range(0, kb, 2 * SUBLANES):
                p = jnp.exp2(s_scr[mp, r0:r0 + 2 * SUBLANES, :] - shift)
                lsum = lsum + p[0:SUBLANES, :] + p[SUBLANES:2 * SUBLANES, :]
                p_scr[mp, r0:r0 + 2 * SUBLANES, :] = p.astype(BF16)
            l_scr[mp] = alpha * l_scr[mp] + jnp.sum(lsum, axis=0, keepdims=True)
            alpha_scr[mp] = alpha
            m_scr[mp] = m_new
            scores(jnp.maximum(j - 1, 0), mp, False)

    for mp in range(2):
        scores(qi, mp, True)
    block(qi, True)

    def body(i, _):
        block(qi - 1 - i, False)
        return 0

    lax.fori_loop(0, qi, body, 0)
    for mp in range(2):
        values(0, mp)

    lam = (jnp.exp(jnp.sum(lq1_ref[...] * lk1_ref[...], axis=1, keepdims=True))
           - jnp.exp(jnp.sum(lq2_ref[...] * lk2_ref[...], axis=1, keepdims=True)) + lambda_init)
    y = acc_scr[0] / l_scr[0] - lam * (acc_scr[1] / l_scr[1])
    y = y * lax.rsqrt(jnp.mean(y * y, axis=0, keepdims=True) + EPS)
    o_ref[0, 0] = (y * subln_ref[...] * (1.0 - lambda_init)).astype(o_ref.dtype)


def _da_call(slopes, lq1, lk1, lq2, lk2, subln, qT, k_all, vT, *, k_col0, kb, tq, lambda_init):
    bsz, nh, dq2, s = qT.shape
    dv = vT.shape[3]
    nkb = s // kb
    kern = functools.partial(_da_kernel, kb=kb, tq=tq, lambda_init=lambda_init)
    kblk0 = k_col0 // dq2
    vec = lambda a: pl.BlockSpec(a.shape, lambda b, h, i: (0, 0))
    return pl.pallas_call(
        kern,
        grid=(bsz, nh, s // tq),
        in_specs=[pl.BlockSpec(memory_space=pltpu.SMEM),
                  vec(lq1), vec(lk1), vec(lq2), vec(lk2), vec(subln),
                  pl.BlockSpec((1, 1, dq2, tq), lambda b, h, i: (b, h, 0, i)),
                  pl.BlockSpec((1, s, dq2), lambda b, h, i: (b, 0, kblk0 + h)),
                  pl.BlockSpec((1, 1, nkb, dv, kb), lambda b, h, i: (b, h, 0, 0, 0))],
        out_specs=pl.BlockSpec((1, 1, dv, tq), lambda b, h, i: (b, h, 0, i)),
        out_shape=jax.ShapeDtypeStruct((bsz, nh, dv, s), BF16),
        scratch_shapes=[pltpu.VMEM((2, kb, tq), F32), pltpu.VMEM((2, kb, tq), BF16), pltpu.VMEM((kb, LANES), F32),
                        pltpu.VMEM((2, SUBLANES, tq), F32), pltpu.VMEM((2, 1, tq), F32), pltpu.VMEM((2, 1, tq), F32), pltpu.VMEM((2, 1, tq), F32), pltpu.VMEM((2, dv, tq), F32)],
        compiler_params=pltpu.CompilerParams(dimension_semantics=("arbitrary",) * 3,
                                             vmem_limit_bytes=VMEM_LIMIT),
        name="diff_attn",
    )(slopes, lq1, lk1, lq2, lk2, subln, qT, k_all, vT)


def _merge_ffn_kernel(h_ref, mod_ref, npre_ref, npost_ref, ya_ref, yd_ref, gate_ref, wsb_ref, wda_ref, wout_ref,
                      wg_ref, wu_ref, wd_ref, o_ref, *, fc):
    h = h_ref[0]
    mod = mod_ref[0]
    d = h.shape[1]
    a = _dot(ya_ref[0], wsb_ref[...])
    b = _dot(yd_ref[0], wda_ref[...])
    merged = gate_ref[0, :, 0:d].astype(F32) * a + gate_ref[0, :, d:2 * d].astype(F32) * b
    m = _dot(merged.astype(BF16), wout_ref[...])
    h2 = h + mod[5:6, :] * _rms(m, npost_ref[1:2, :])
    o_ref[0] = _swiglu_sublayer(h2, mod, npre_ref[...], npost_ref[...], wg_ref, wu_ref, wd_ref, 2, 0.5, fc)


def _merge_ffn_call(h, mod, npre, npost, ya, yd, gates, wsb, wda, wout, wg, wu, wd, *, tm):
    bsz, s, d = h.shape
    kern = functools.partial(_merge_ffn_kernel, fc=_ffn_chunk(wg.shape[1]))
    tok = lambda w: pl.BlockSpec((1, tm, w), lambda b, i: (b, i, 0))
    return pl.pallas_call(
        kern,
        grid=(bsz, s // tm),
        in_specs=[tok(d), pl.BlockSpec((1, 3 * N_SUB, d), lambda b, i: (b, 0, 0)),
                  _const_spec(npre.shape), _const_spec(npost.shape),
                  tok(ya.shape[2]), tok(yd.shape[2]), tok(gates.shape[2]),
                  _const_spec(wsb.shape), _const_spec(wda.shape), _const_spec(wout.shape),
                  _const_spec(wg.shape), _const_spec(wu.shape), _const_spec(wd.shape)],
        out_specs=tok(d),
        out_shape=jax.ShapeDtypeStruct(h.shape, F32),
        compiler_params=pltpu.CompilerParams(dimension_semantics=("arbitrary", "arbitrary"),
                                             vmem_limit_bytes=VMEM_LIMIT),
        name="merge_out_ffn2",
    )(h, mod, npre, npost, ya, yd, gates, wsb, wda, wout, wg, wu, wd)


def _layer(h, c, w_ada, b_ada, npre, npost, f1g, f1u, f1d, w_in, lq1, lk1, lq2, lk2, subln,
           w_sb, w_da, w_out, f2g, f2u, f2d, lambda_init):
    bsz, s, d = h.shape
    tm = min(512, s)
    kb = tq = min(512, s)
    nkb = s // kb
    sbw = SB_HEADS * SB_HEAD_DIM
    daw = DA_HEADS * 2 * DA_HEAD_DIM
    dvw = DA_HEADS * DA_V_DIM
    n_feat = 3 * sbw + 2 * daw + dvw
    bf = lambda w: w.astype(BF16)

    mod = _mod_call(c, w_ada, b_ada).reshape(bsz, 3 * N_SUB, d)
    h1 = _ffn_call(h, mod, npre, npost, bf(f1g), bf(f1u), bf(f1d), sub=0, resid_w=0.5, tm=tm)
    q_cols = ((0, sbw), (3 * sbw, 3 * sbw + daw))
    feat, gates = _proj_call(h1, mod, npre, bf(w_in), n_feat=n_feat, q_cols=q_cols,
                             q_scale=LOG2E / math.sqrt(SB_HEAD_DIM), tm=tm)

    qa = feat[..., 0:sbw].reshape(bsz, s, SB_HEADS, SB_HEAD_DIM).transpose(0, 2, 3, 1)
    ka = feat[..., sbw:2 * sbw].reshape(bsz, nkb, SUBLANES, kb // SUBLANES, SB_HEADS, SB_HEAD_DIM)
    ka = ka.transpose(0, 4, 1, 3, 2, 5).reshape(bsz, SB_HEADS, nkb, kb, SB_HEAD_DIM)
    va = feat[..., 2 * sbw:3 * sbw].reshape(bsz, nkb, SUBLANES, kb // SUBLANES, SB_HEADS, SB_HEAD_DIM)
    va = va.transpose(0, 4, 1, 5, 3, 2).reshape(bsz, SB_HEADS, nkb, SB_HEAD_DIM, kb)
    qd = feat[..., 3 * sbw:3 * sbw + daw].reshape(bsz, s, DA_HEADS, 2 * DA_HEAD_DIM).transpose(0, 2, 3, 1)
    vd = feat[..., 3 * sbw + 2 * daw:].reshape(bsz, nkb, kb, DA_HEADS, DA_V_DIM).transpose(0, 3, 1, 4, 2)

    yaT = _sb_call(qa, ka, va, kb=kb, tq=tq)
    slopes = jnp.asarray(LOG2E * 2.0 ** (-8.0 * (np.arange(DA_HEADS) + 1) / DA_HEADS), dtype=F32)
    row = lambda v: v.reshape(1, -1).astype(F32)
    ydT = _da_call(slopes, row(lq1), row(lk1), row(lq2), row(lk2), subln.reshape(-1, 1).astype(F32),
                   qd, feat, vd, k_col0=3 * sbw + daw, kb=kb, tq=tq, lambda_init=lambda_init)
    ya = yaT.transpose(0, 3, 1, 2).reshape(bsz, s, sbw)
    yd = ydT.transpose(0, 3, 1, 2).reshape(bsz, s, dvw)
    return _merge_ffn_call(h1, mod, npre, npost, ya, yd, gates, bf(w_sb), bf(w_da), bf(w_out),
                           bf(f2g), bf(f2u), bf(f2d), tm=tm)


def kernel(x, c, w_ada, b_ada, norm_pre, norm_post, ffn1_w_gate, ffn1_w_up, ffn1_w_down, w_in, da_lambda_q1, da_lambda_k1, da_lambda_q2, da_lambda_k2, da_subln, w_branch_sb, w_branch_da, w_out, ffn2_w_gate, ffn2_w_up, ffn2_w_down):
    h = x
    for l in range(w_ada.shape[0]):
        lambda_init = 0.8 - 0.6 * math.exp(-0.3 * l)
        h = _layer(h, c, w_ada[l], b_ada[l], norm_pre[l], norm_post[l],
                   ffn1_w_gate[l], ffn1_w_up[l], ffn1_w_down[l], w_in[l],
                   da_lambda_q1[l], da_lambda_k1[l], da_lambda_q2[l], da_lambda_k2[l], da_subln[l],
                   w_branch_sb[l], w_branch_da[l], w_out[l],
                   ffn2_w_gate[l], ffn2_w_up[l], ffn2_w_down[l], lambda_init)
    return h
```

```python
import functools
import math

import jax
import jax.numpy as jnp
import numpy as np
from jax import lax
from jax.experimental import pallas as pl
from jax.experimental.pallas import tpu as pltpu

EPS = 1e-6
NEG = -1e30
N_SUB = 3
CHUNK_LOG2 = 6
SB_HEADS = 8
SB_HEAD_DIM = 64
DA_HEADS = 4
DA_HEAD_DIM = 64
DA_V_DIM = 2 * DA_HEAD_DIM
SUBLANES = 8
BF16_ROWS = 16
MXU_DIM = 256
LANES = 128
SB_QK_ROWS = 128
DA_QK_ROWS = 128
LOG2E = 1.4426950408889634
VMEM_LIMIT = 56 * 1024 * 1024

F32 = jnp.float32
BF16 = jnp.bfloat16


def _rms(x, g):
    return x * lax.rsqrt(jnp.mean(x * x, axis=-1, keepdims=True) + EPS) * g


def _dot(a, b):
    return jnp.dot(a, b, preferred_element_type=F32)


def _mod_kernel(c_ref, w_ref, b_ref, o_ref):
    c = c_ref[...]
    a = c * jax.nn.sigmoid(c)
    o_ref[...] = jnp.dot(a, w_ref[...], preferred_element_type=F32,
                         precision=lax.Precision.HIGHEST) + b_ref[...]


def _mod_call(c, w, b):
    bsz, d = c.shape
    n = w.shape[1]
    tn = 1024 if n % 1024 == 0 else n
    return pl.pallas_call(
        _mod_kernel,
        grid=(n // tn,),
        in_specs=[pl.BlockSpec((bsz, d), lambda j: (0, 0)),
                  pl.BlockSpec((d, tn), lambda j: (0, j)),
                  pl.BlockSpec((1, tn), lambda j: (0, j))],
        out_specs=pl.BlockSpec((bsz, tn), lambda j: (0, j)),
        out_shape=jax.ShapeDtypeStruct((bsz, n), F32),
        name="adaln_mod",
    )(c, w, b.reshape(1, n))


def _swiglu_sublayer(h, mod, npre, npost, wg_ref, wu_ref, wd_ref, sub, resid_w, fc):
    shift = mod[3 * sub + 0:3 * sub + 1, :]
    scale = mod[3 * sub + 1:3 * sub + 2, :]
    gate = mod[3 * sub + 2:3 * sub + 3, :]
    u = (_rms(h, npre[sub:sub + 1, :]) * (1.0 + scale) + shift).astype(BF16)
    f = wg_ref.shape[1]
    acc = jnp.zeros(h.shape, F32)
    for c0 in range(0, f, fc):
        g = _dot(u, wg_ref[:, c0:c0 + fc])
        up = _dot(u, wu_ref[:, c0:c0 + fc])
        a = (g * jax.nn.sigmoid(g) * up).astype(BF16)
        acc = acc + _dot(a, wd_ref[c0:c0 + fc, :])
    return h + resid_w * gate * _rms(acc, npost[sub:sub + 1, :])


def _ffn_kernel(h_ref, mod_ref, npre_ref, npost_ref, wg_ref, wu_ref, wd_ref, o_ref, *, sub, resid_w, fc):
    o_ref[0] = _swiglu_sublayer(h_ref[0], mod_ref[0], npre_ref[...], npost_ref[...],
                                wg_ref, wu_ref, wd_ref, sub, resid_w, fc)


def _const_spec(shape):
    nd = len(shape)
    return pl.BlockSpec(shape, lambda *_: (0,) * nd, pipeline_mode=pl.Buffered(1))


def _ffn_chunk(f):
    for fc in (256, 128):
        if f % fc == 0:
            return fc
    return f


def _ffn_call(h, mod, npre, npost, wg, wu, wd, *, sub, resid_w, tm):
    bsz, s, d = h.shape
    f = wg.shape[1]
    kern = functools.partial(_ffn_kernel, sub=sub, resid_w=resid_w, fc=_ffn_chunk(f))
    return pl.pallas_call(
        kern,
        grid=(bsz, s // tm),
        in_specs=[pl.BlockSpec((1, tm, d), lambda b, i: (b, i, 0)),
                  pl.BlockSpec((1, 3 * N_SUB, d), lambda b, i: (b, 0, 0)),
                  _const_spec(npre.shape), _const_spec(npost.shape),
                  _const_spec(wg.shape), _const_spec(wu.shape), _const_spec(wd.shape)],
        out_specs=pl.BlockSpec((1, tm, d), lambda b, i: (b, i, 0)),
        out_shape=jax.ShapeDtypeStruct(h.shape, F32),
        compiler_params=pltpu.CompilerParams(dimension_semantics=("arbitrary", "arbitrary"),
                                             vmem_limit_bytes=VMEM_LIMIT),
        name="ffn1",
    )(h, mod, npre, npost, wg, wu, wd)


_NT = (((1,), (1,)), ((), ()))
_TN = (((0,), (0,)), ((), ()))


def _proj_kernel(h_ref, mod_ref, npre_ref, wqv_ref, wk_ref, wgate_ref, perm_ref, permT_ref,
                 qT_ref, vT_ref, k_ref, gate_ref, *, sub, q_scale, sb_width, nc):
    h = h_ref[0]
    mod = mod_ref[0]
    shift = mod[3 * sub + 0:3 * sub + 1, :]
    scale = mod[3 * sub + 1:3 * sub + 2, :]
    u = (_rms(h, npre_ref[sub:sub + 1, :]) * (1.0 + scale) + shift).astype(BF16)
    nq = qT_ref.shape[1]
    for c0 in range(0, nq, nc):
        p = lax.dot_general(wqv_ref[c0:c0 + nc, :], u, _NT, preferred_element_type=F32)
        qT_ref[0, c0:c0 + nc, :] = (p * q_scale).astype(BF16)
    for c0 in range(0, vT_ref.shape[2], nc):
        p = lax.dot_general(wqv_ref[nq + c0:nq + c0 + nc, :], u, _NT, preferred_element_type=F32).astype(BF16)
        if c0 < sb_width:
            p = _dot(p, permT_ref[...]).astype(BF16)
        vT_ref[0, 0, c0:c0 + nc, :] = p
    for c0 in range(0, k_ref.shape[2], nc):
        p = _dot(u, wk_ref[:, c0:c0 + nc]).astype(BF16)
        if c0 < sb_width:
            p = _dot(perm_ref[...], p).astype(BF16)
        k_ref[0, :, c0:c0 + nc] = p
    for c0 in range(0, gate_ref.shape[2], nc):
        gate_ref[0, :, c0:c0 + nc] = jax.nn.sigmoid(_dot(u, wgate_ref[:, c0:c0 + nc])).astype(BF16)


def _proj_call(h, mod, npre, wqv, wk, wgate, perm, *, q_scale, sb_width, n_q, tm):
    bsz, s, d = h.shape
    n_v = wqv.shape[0] - n_q
    kern = functools.partial(_proj_kernel, sub=1, q_scale=q_scale, sb_width=sb_width, nc=512)
    return pl.pallas_call(
        kern,
        grid=(bsz, s // tm),
        in_specs=[pl.BlockSpec((1, tm, d), lambda b, i: (b, i, 0)),
                  pl.BlockSpec((1, 3 * N_SUB, d), lambda b, i: (b, 0, 0)),
                  _const_spec(npre.shape), _const_spec(wqv.shape), _const_spec(wk.shape),
                  _const_spec(wgate.shape), _const_spec(perm.shape), _const_spec(perm.shape)],
        out_specs=[pl.BlockSpec((1, n_q, tm), lambda b, i: (b, 0, i)),
                   pl.BlockSpec((1, 1, n_v, tm), lambda b, i: (b, i, 0, 0)),
                   pl.BlockSpec((1, tm, wk.shape[1]), lambda b, i: (b, i, 0)),
                   pl.BlockSpec((1, tm, wgate.shape[1]), lambda b, i: (b, i, 0))],
        out_shape=[jax.ShapeDtypeStruct((bsz, n_q, s), BF16),
                   jax.ShapeDtypeStruct((bsz, s // tm, n_v, tm), BF16),
                   jax.ShapeDtypeStruct((bsz, s, wk.shape[1]), BF16),
                   jax.ShapeDtypeStruct((bsz, s, wgate.shape[1]), BF16)],
        compiler_params=pltpu.CompilerParams(dimension_semantics=("arbitrary", "arbitrary"),
                                             vmem_limit_bytes=VMEM_LIMIT),
        name="mixer_in_proj",
    )(h, mod, npre, wqv, wk, wgate, perm, perm.T)


def _neg_abs(z):
    bits = lax.bitcast_convert_type(z, jnp.uint32) | jnp.uint32(0x80000000)
    return lax.bitcast_convert_type(bits, F32)


def _sb_kernel(qT_ref, k_ref, vT_ref, o_ref, z_scr, y_scr, w_scr, acc_scr, *, kb, tq):
    qi = pl.program_id(2)
    seg = kb // SUBLANES
    qk_rows = min(SB_QK_ROWS, kb)
    pv_rows = min(MXU_DIM, kb)
    qT = qT_ref[0]
    own = (lax.broadcasted_iota(jnp.int32, qT.shape, 0) // SB_HEAD_DIM) == (pl.program_id(1) % 2)
    qT = jnp.where(own, qT, jnp.zeros_like(qT))
    acc_scr[...] = jnp.zeros_like(acc_scr)

    def scores(j):
        for c0 in range(kb - qk_rows, -1, -qk_rows):
            rows = pl.ds(pl.multiple_of(j * kb + c0, qk_rows), qk_rows)
            z_scr[c0:c0 + qk_rows, :] = _dot(k_ref[0, rows, :], qT)

    def values(j):
        for c0 in range(0, kb, pv_rows):
            acc_scr[...] += _dot(vT_ref[0, j, :, c0:c0 + pv_rows], w_scr[c0:c0 + pv_rows, :])

    def block(j, carry, first):
        if first:
            q_local = lax.broadcasted_iota(jnp.int32, (SUBLANES, tq), 1)
            seg_start = lax.broadcasted_iota(jnp.int32, (SUBLANES, tq), 0) * seg
        else:
            values(j + 1)
        for r0 in range(kb - 2 * SUBLANES, -1, -2 * SUBLANES):
            z = z_scr[r0:r0 + 2 * SUBLANES, :]
            sp = jnp.maximum(z, 0.0) + LOG2E * jnp.log(1.0 + jnp.exp2(_neg_abs(z)))
            if first:
                v = r0 // SUBLANES
                before = jnp.concatenate([seg_start + v < q_local, seg_start + (v + 1) < q_local], axis=0)
                sp = jnp.where(before, sp, 0.0)
            y_scr[r0:r0 + 2 * SUBLANES, :] = sp
        run = jnp.zeros((SUBLANES, tq), F32)
        for r0 in range(kb - SUBLANES, -1, -SUBLANES):
            run = run + y_scr[r0:r0 + SUBLANES, :]
            y_scr[r0:r0 + SUBLANES, :] = z_scr[r0:r0 + SUBLANES, :] - run
        s = carry
        offs = [None] * SUBLANES
        for r in range(SUBLANES - 1, -1, -1):
            offs[r] = s
            s = s + run[r:r + 1, :]
        off = jnp.concatenate(offs, axis=0)
        scores(jnp.maximum(j - 1, 0))
        off2 = jnp.concatenate([off, off], axis=0)
        for r0 in range(0, kb, 2 * SUBLANES):
            logw = y_scr[r0:r0 + 2 * SUBLANES, :] - off2
            if first:
                v = r0 // SUBLANES
                before = jnp.concatenate([seg_start + v < q_local, seg_start + (v + 1) < q_local], axis=0)
                logw = jnp.where(before, logw, NEG)
            w_scr[r0:r0 + 2 * SUBLANES, :] = jnp.exp2(logw.astype(BF16))
        return s

    scores(qi)
    carry = block(qi, jnp.zeros((1, tq), F32), True)
    lax.fori_loop(0, qi, lambda i, cr: block(qi - 1 - i, cr, False), carry)
    values(0)
    o_ref[0] = acc_scr[...].astype(o_ref.dtype)


def _sb_call(qT, k, vT, *, kb, tq):
    bsz, _, s = qT.shape
    nkb = s // kb
    dh, pair = SB_HEAD_DIM, 2 * SB_HEAD_DIM
    kern = functools.partial(_sb_kernel, kb=kb, tq=tq)
    return pl.pallas_call(
        kern,
        grid=(bsz, SB_HEADS, s // tq),
        in_specs=[pl.BlockSpec((1, pair, tq), lambda b, h, i: (b, h // 2, i)),
                  pl.BlockSpec((1, s, pair), lambda b, h, i: (b, 0, h // 2)),
                  pl.BlockSpec((1, nkb, dh, kb), lambda b, h, i: (b, 0, h, 0))],
        out_specs=pl.BlockSpec((1, dh, tq), lambda b, h, i: (b, h, i)),
        out_shape=jax.ShapeDtypeStruct((bsz, SB_HEADS * dh, s), BF16),
        scratch_shapes=[pltpu.VMEM((kb, tq), F32), pltpu.VMEM((kb, tq), F32), pltpu.VMEM((kb, tq), BF16),
                        pltpu.VMEM((dh, tq), F32)],
        compiler_params=pltpu.CompilerParams(dimension_semantics=("arbitrary",) * 3,
                                             vmem_limit_bytes=VMEM_LIMIT),
        name="stickbreak_attn",
    )(qT, k, vT)


def _da_kernel(slope_ref, lq1_ref, lk1_ref, lq2_ref, lk2_ref, subln_ref, qT_ref, k_ref, vT_ref, o_ref,
               s_scr, p_scr, base_scr, mx_scr, alpha_scr, m_scr, acc_scr, *, kb, tq, lambda_init):
    h = pl.program_id(1)
    qi = pl.program_id(2)
    slope = slope_ref[h]
    dh = qT_ref.shape[1] // 2
    qk_rows = min(DA_QK_ROWS, kb)
    pv_rows = min(MXU_DIM, kb)
    lane_blocks = tq // LANES
    qT = qT_ref[0]
    qrow = lax.broadcasted_iota(jnp.int32, qT.shape, 0)
    q_pad = (jnp.where(qrow < dh, qT, jnp.zeros_like(qT)), jnp.where(qrow >= dh, qT, jnp.zeros_like(qT)))
    m_scr[...] = jnp.full_like(m_scr, NEG)
    acc_scr[...] = jnp.zeros_like(acc_scr)
    base_scr[...] = slope * lax.broadcasted_iota(jnp.int32, (kb, LANES), 0).astype(F32)

    def scores(j, mp, diagonal):
        mx = jnp.full((SUBLANES, tq), NEG, F32)
        for c0 in range(0, kb, qk_rows):
            rows = pl.ds(pl.multiple_of(j * kb + c0, qk_rows), qk_rows)
            zc = _dot(k_ref[0, rows, :], q_pad[mp])
            if diagonal:
                kl = c0 + lax.broadcasted_iota(jnp.int32, (qk_rows, tq), 0)
                tl = lax.broadcasted_iota(jnp.int32, (qk_rows, tq), 1)
                zc = zc + slope * (tl - jnp.abs(tl - kl)).astype(F32)
                zc = jnp.where((kl >> CHUNK_LOG2) <= (tl >> CHUNK_LOG2), zc, NEG)
            else:
                zc = zc + jnp.concatenate([base_scr[c0:c0 + qk_rows, :]] * lane_blocks, axis=1)
            s_scr[mp, c0:c0 + qk_rows, :] = zc
            for r0 in range(0, qk_rows, SUBLANES):
                mx = jnp.maximum(mx, zc[r0:r0 + SUBLANES, :])
        mx_scr[mp] = mx

    def values(j, mp):
        pv = None
        for c0 in range(0, kb, pv_rows):
            lhs = jnp.concatenate([vT_ref[0, j, :, c0:c0 + pv_rows], jnp.ones((BF16_ROWS, pv_rows), BF16)], axis=0)
            d = _dot(lhs, p_scr[mp, c0:c0 + pv_rows, :])
            pv = d if pv is None else pv + d
        acc_scr[mp] = alpha_scr[mp] * acc_scr[mp] + pv

    def block(j, first):
        cj = 0.0 if first else slope * ((j - qi) * kb).astype(F32)
        if not first:
            for mp in range(2):
                values(j + 1, mp)
        for mp in range(2):
            m_old = m_scr[mp]
            m_new = jnp.maximum(m_old, jnp.max(mx_scr[mp], axis=0, keepdims=True) + cj)
            alpha = jnp.exp2(m_old - m_new)
            shift = jnp.broadcast_to(m_new - cj, (BF16_ROWS, tq))
            for r0 in range(0, kb, BF16_ROWS):
                p_scr[mp, r0:r0 + BF16_ROWS, :] = jnp.exp2((s_scr[mp, r0:r0 + BF16_ROWS, :] - shift).astype(BF16))
            alpha_scr[mp] = alpha
            m_scr[mp] = m_new
            scores(jnp.maximum(j - 1, 0), mp, False)

    for mp in range(2):
        scores(qi, mp, True)
    block(qi, True)

    def body(i, _):
        block(qi - 1 - i, False)
        return 0

    lax.fori_loop(0, qi, body, 0)
    for mp in range(2):
        values(0, mp)

    lam = (jnp.exp(jnp.sum(lq1_ref[...] * lk1_ref[...], axis=1, keepdims=True))
           - jnp.exp(jnp.sum(lq2_ref[...] * lk2_ref[...], axis=1, keepdims=True)) + lambda_init)
    dv = o_ref.shape[1]
    y = (acc_scr[0, 0:dv, :] / acc_scr[0, dv:dv + 1, :]
         - lam * (acc_scr[1, 0:dv, :] / acc_scr[1, dv:dv + 1, :]))
    y = y * lax.rsqrt(jnp.mean(y * y, axis=0, keepdims=True) + EPS)
    o_ref[0] = (y * subln_ref[...] * (1.0 - lambda_init)).astype(o_ref.dtype)


def _da_call(slopes, lq1, lk1, lq2, lk2, subln, qT, k, vT, *, blk0, kb, tq, lambda_init):
    bsz, _, s = qT.shape
    dq2, dv = 2 * DA_HEAD_DIM, DA_V_DIM
    nkb = s // kb
    kern = functools.partial(_da_kernel, kb=kb, tq=tq, lambda_init=lambda_init)
    vec = lambda a: pl.BlockSpec(a.shape, lambda b, h, i: (0, 0))
    return pl.pallas_call(
        kern,
        grid=(bsz, DA_HEADS, s // tq),
        in_specs=[pl.BlockSpec(memory_space=pltpu.SMEM),
                  vec(lq1), vec(lk1), vec(lq2), vec(lk2), vec(subln),
                  pl.BlockSpec((1, dq2, tq), lambda b, h, i: (b, blk0 + h, i)),
                  pl.BlockSpec((1, s, dq2), lambda b, h, i: (b, 0, blk0 + h)),
                  pl.BlockSpec((1, nkb, dv, kb), lambda b, h, i: (b, 0, blk0 + h, 0))],
        out_specs=pl.BlockSpec((1, dv, tq), lambda b, h, i: (b, h, i)),
        out_shape=jax.ShapeDtypeStruct((bsz, DA_HEADS * dv, s), BF16),
        scratch_shapes=[pltpu.VMEM((2, kb, tq), F32), pltpu.VMEM((2, kb, tq), BF16), pltpu.VMEM((kb, LANES), F32),
                        pltpu.VMEM((2, SUBLANES, tq), F32), pltpu.VMEM((2, 1, tq), F32), pltpu.VMEM((2, 1, tq), F32),
                        pltpu.VMEM((2, dv + BF16_ROWS, tq), F32)],
        compiler_params=pltpu.CompilerParams(dimension_semantics=("arbitrary",) * 3,
                                             vmem_limit_bytes=VMEM_LIMIT),
        name="diff_attn",
    )(slopes, lq1, lk1, lq2, lk2, subln, qT, k, vT)


def _merge_ffn_kernel(h_ref, mod_ref, npre_ref, npost_ref, ya_ref, yd_ref, gate_ref, wsb_ref, wda_ref, wout_ref,
                      wg_ref, wu_ref, wd_ref, o_ref, *, fc):
    h = h_ref[0]
    mod = mod_ref[0]
    d = h.shape[1]
    a = lax.dot_general(ya_ref[0], wsb_ref[...], _TN, preferred_element_type=F32)
    b = lax.dot_general(yd_ref[0], wda_ref[...], _TN, preferred_element_type=F32)
    merged = gate_ref[0, :, 0:d].astype(F32) * a + gate_ref[0, :, d:2 * d].astype(F32) * b
    m = _dot(merged.astype(BF16), wout_ref[...])
    h2 = h + mod[5:6, :] * _rms(m, npost_ref[1:2, :])
    o_ref[0] = _swiglu_sublayer(h2, mod, npre_ref[...], npost_ref[...], wg_ref, wu_ref, wd_ref, 2, 0.5, fc)


def _merge_ffn_call(h, mod, npre, npost, ya, yd, gates, wsb, wda, wout, wg, wu, wd, *, tm):
    bsz, s, d = h.shape
    kern = functools.partial(_merge_ffn_kernel, fc=_ffn_chunk(wg.shape[1]))
    tok = lambda w: pl.BlockSpec((1, tm, w), lambda b, i: (b, i, 0))
    return pl.pallas_call(
        kern,
        grid=(bsz, s // tm),
        in_specs=[tok(d), pl.BlockSpec((1, 3 * N_SUB, d), lambda b, i: (b, 0, 0)),
                  _const_spec(npre.shape), _const_spec(npost.shape),
                  pl.BlockSpec((1, ya.shape[1], tm), lambda b, i: (b, 0, i)),
                  pl.BlockSpec((1, yd.shape[1], tm), lambda b, i: (b, 0, i)), tok(gates.shape[2]),
                  _const_spec(wsb.shape), _const_spec(wda.shape), _const_spec(wout.shape),
                  _const_spec(wg.shape), _const_spec(wu.shape), _const_spec(wd.shape)],
        out_specs=tok(d),
        out_shape=jax.ShapeDtypeStruct(h.shape, F32),
        compiler_params=pltpu.CompilerParams(dimension_semantics=("arbitrary", "arbitrary"),
                                             vmem_limit_bytes=VMEM_LIMIT),
        name="merge_out_ffn2",
    )(h, mod, npre, npost, ya, yd, gates, wsb, wda, wout, wg, wu, wd)


def _sb_permutation(kb):
    rho = np.arange(kb)
    p = np.zeros((kb, kb), np.float32)
    p[rho, (rho % SUBLANES) * (kb // SUBLANES) + rho // SUBLANES] = 1.0
    return jnp.asarray(p, dtype=BF16)


def _layer(h, c, w_ada, b_ada, npre, npost, f1g, f1u, f1d, w_in, lq1, lk1, lq2, lk2, subln,
           w_sb, w_da, w_out, f2g, f2u, f2d, lambda_init):
    bsz, s, d = h.shape
    tm = kb = tq = min(512, s)
    sbw = SB_HEADS * SB_HEAD_DIM
    daw = DA_HEADS * 2 * DA_HEAD_DIM
    dvw = DA_HEADS * DA_V_DIM
    assert sbw == daw == dvw, "feature blocks of q / k / v are addressed with one block offset"
    bf = lambda w: w.astype(BF16)
    o = np.cumsum([0, sbw, sbw, sbw, daw, daw, dvw])
    col = lambda i: w_in[:, o[i]:o[i + 1]]
    wqv = bf(jnp.concatenate([col(0), col(3), col(2), col(5)], axis=1).T)
    wk = bf(jnp.concatenate([col(1), col(4)], axis=1))
    wgate = bf(w_in[:, o[6]:])

    mod = _mod_call(c, w_ada, b_ada).reshape(bsz, 3 * N_SUB, d)
    h1 = _ffn_call(h, mod, npre, npost, bf(f1g), bf(f1u), bf(f1d), sub=0, resid_w=0.5, tm=tm)
    qT, vT, k, gates = _proj_call(h1, mod, npre, wqv, wk, wgate, _sb_permutation(kb),
                                  q_scale=LOG2E / math.sqrt(SB_HEAD_DIM), sb_width=sbw, n_q=sbw + daw, tm=tm)
    yaT = _sb_call(qT, k, vT, kb=kb, tq=tq)
    slopes = jnp.asarray(LOG2E * 2.0 ** (-8.0 * (np.arange(DA_HEADS) + 1) / DA_HEADS), dtype=F32)
    row = lambda v: v.reshape(1, -1).astype(F32)
    ydT = _da_call(slopes, row(lq1), row(lk1), row(lq2), row(lk2), subln.reshape(-1, 1).astype(F32),
                   qT, k, vT, blk0=sbw // (2 * DA_HEAD_DIM), kb=kb, tq=tq, lambda_init=lambda_init)
    return _merge_ffn_call(h1, mod, npre, npost, yaT, ydT, gates, bf(w_sb), bf(w_da), bf(w_out),
                           bf(f2g), bf(f2u), bf(f2d), tm=tm)


def kernel(x, c, w_ada, b_ada, norm_pre, norm_post, ffn1_w_gate, ffn1_w_up, ffn1_w_down, w_in, da_lambda_q1, da_lambda_k1, da_lambda_q2, da_lambda_k2, da_subln, w_branch_sb, w_branch_da, w_out, ffn2_w_gate, ffn2_w_up, ffn2_w_down):
    h = x
    for l in range(w_ada.shape[0]):
        lambda_init = 0.8 - 0.6 * math.exp(-0.3 * l)
        h = _layer(h, c, w_ada[l], b_ada[l], norm_pre[l], norm_post[l],
                   ffn1_w_gate[l], ffn1_w_up[l], ffn1_w_down[l], w_in[l],
                   da_lambda_q1[l], da_lambda_k1[l], da_lambda_q2[l], da_lambda_k2[l], da_subln[l],
                   w_branch_sb[l], w_branch_da[l], w_out[l],
                   ffn2_w_gate[l], ffn2_w_up[l], ffn2_w_down[l], lambda_init)
    return h
```

```python
import functools
import math

import jax
import jax.numpy as jnp
import numpy as np
from jax import lax
from jax.experimental import pallas as pl
from jax.experimental.pallas import tpu as pltpu

EPS = 1e-6
NEG = -1e30
N_SUB = 3
CHUNK_LOG2 = 6
SB_HEADS = 8
SB_HEAD_DIM = 64
DA_HEADS = 4
DA_HEAD_DIM = 64
DA_V_DIM = 2 * DA_HEAD_DIM
SUBLANES = 8
BF16_ROWS = 16
MXU_DIM = 256
LANES = 128
SB_QK_ROWS = 128
DA_QK_ROWS = 512
LOG2E = 1.4426950408889634
VMEM_LIMIT = 56 * 1024 * 1024

F32 = jnp.float32
BF16 = jnp.bfloat16


def _rms(x, g):
    return x * lax.rsqrt(jnp.mean(x * x, axis=-1, keepdims=True) + EPS) * g


def _dot(a, b):
    return jnp.dot(a, b, preferred_element_type=F32)


def _mod_kernel(c_ref, w_ref, b_ref, o_ref):
    c = c_ref[...]
    a = c * jax.nn.sigmoid(c)
    o_ref[...] = jnp.dot(a, w_ref[...], preferred_element_type=F32,
                         precision=lax.Precision.HIGHEST) + b_ref[...]


def _mod_call(c, w, b):
    bsz, d = c.shape
    n = w.shape[1]
    tn = 1024 if n % 1024 == 0 else n
    return pl.pallas_call(
        _mod_kernel,
        grid=(n // tn,),
        in_specs=[pl.BlockSpec((bsz, d), lambda j: (0, 0)),
                  pl.BlockSpec((d, tn), lambda j: (0, j)),
                  pl.BlockSpec((1, tn), lambda j: (0, j))],
        out_specs=pl.BlockSpec((bsz, tn), lambda j: (0, j)),
        out_shape=jax.ShapeDtypeStruct((bsz, n), F32),
        name="adaln_mod",
    )(c, w, b.reshape(1, n))


def _swiglu_sublayer(h, mod, npre, npost, wg_ref, wu_ref, wd_ref, sub, resid_w, fc):
    shift = mod[3 * sub + 0:3 * sub + 1, :]
    scale = mod[3 * sub + 1:3 * sub + 2, :]
    gate = mod[3 * sub + 2:3 * sub + 3, :]
    u = (_rms(h, npre[sub:sub + 1, :]) * (1.0 + scale) + shift).astype(BF16)
    f = wg_ref.shape[1]
    acc = jnp.zeros(h.shape, F32)
    for c0 in range(0, f, fc):
        g = _dot(u, wg_ref[:, c0:c0 + fc])
        up = _dot(u, wu_ref[:, c0:c0 + fc])
        a = (g * jax.nn.sigmoid(g) * up).astype(BF16)
        acc = acc + _dot(a, wd_ref[c0:c0 + fc, :])
    return h + resid_w * gate * _rms(acc, npost[sub:sub + 1, :])


def _ffn_kernel(h_ref, mod_ref, npre_ref, npost_ref, wg_ref, wu_ref, wd_ref, o_ref, *, sub, resid_w, fc):
    o_ref[0] = _swiglu_sublayer(h_ref[0], mod_ref[0], npre_ref[...], npost_ref[...],
                                wg_ref, wu_ref, wd_ref, sub, resid_w, fc)


def _const_spec(shape):
    nd = len(shape)
    return pl.BlockSpec(shape, lambda *_: (0,) * nd, pipeline_mode=pl.Buffered(1))


def _ffn_chunk(f):
    for fc in (256, 128):
        if f % fc == 0:
            return fc
    return f


def _ffn_call(h, mod, npre, npost, wg, wu, wd, *, sub, resid_w, tm):
    bsz, s, d = h.shape
    f = wg.shape[1]
    kern = functools.partial(_ffn_kernel, sub=sub, resid_w=resid_w, fc=_ffn_chunk(f))
    return pl.pallas_call(
        kern,
        grid=(bsz, s // tm),
        in_specs=[pl.BlockSpec((1, tm, d), lambda b, i: (b, i, 0)),
                  pl.BlockSpec((1, 3 * N_SUB, d), lambda b, i: (b, 0, 0)),
                  _const_spec(npre.shape), _const_spec(npost.shape),
                  _const_spec(wg.shape), _const_spec(wu.shape), _const_spec(wd.shape)],
        out_specs=pl.BlockSpec((1, tm, d), lambda b, i: (b, i, 0)),
        out_shape=jax.ShapeDtypeStruct(h.shape, F32),
        compiler_params=pltpu.CompilerParams(dimension_semantics=("arbitrary", "arbitrary"),
                                             vmem_limit_bytes=VMEM_LIMIT),
        name="ffn1",
    )(h, mod, npre, npost, wg, wu, wd)


_NT = (((1,), (1,)), ((), ()))
_TN = (((0,), (0,)), ((), ()))


def _proj_kernel(h_ref, mod_ref, npre_ref, wqv_ref, wk_ref, wgate_ref, perm_ref, permT_ref,
                 qT_ref, vT_ref, k_ref, gate_ref, *, sub, q_scale, sb_width, nc):
    h = h_ref[0]
    mod = mod_ref[0]
    shift = mod[3 * sub + 0:3 * sub + 1, :]
    scale = mod[3 * sub + 1:3 * sub + 2, :]
    u = (_rms(h, npre_ref[sub:sub + 1, :]) * (1.0 + scale) + shift).astype(BF16)
    nq = qT_ref.shape[1]
    for c0 in range(0, nq, nc):
        p = lax.dot_general(wqv_ref[c0:c0 + nc, :], u, _NT, preferred_element_type=F32)
        qT_ref[0, c0:c0 + nc, :] = (p * q_scale).astype(BF16)
    for c0 in range(0, vT_ref.shape[2], nc):
        p = lax.dot_general(wqv_ref[nq + c0:nq + c0 + nc, :], u, _NT, preferred_element_type=F32).astype(BF16)
        if c0 < sb_width:
            p = _dot(p, permT_ref[...]).astype(BF16)
        vT_ref[0, 0, c0:c0 + nc, :] = p
    for c0 in range(0, k_ref.shape[2], nc):
        p = _dot(u, wk_ref[:, c0:c0 + nc]).astype(BF16)
        if c0 < sb_width:
            p = _dot(perm_ref[...], p).astype(BF16)
        k_ref[0, :, c0:c0 + nc] = p
    for c0 in range(0, gate_ref.shape[2], nc):
        gate_ref[0, :, c0:c0 + nc] = jax.nn.sigmoid(_dot(u, wgate_ref[:, c0:c0 + nc])).astype(BF16)


def _proj_call(h, mod, npre, wqv, wk, wgate, perm, *, q_scale, sb_width, n_q, tm):
    bsz, s, d = h.shape
    n_v = wqv.shape[0] - n_q
    kern = functools.partial(_proj_kernel, sub=1, q_scale=q_scale, sb_width=sb_width, nc=512)
    return pl.pallas_call(
        kern,
        grid=(bsz, s // tm),
        in_specs=[pl.BlockSpec((1, tm, d), lambda b, i: (b, i, 0)),
                  pl.BlockSpec((1, 3 * N_SUB, d), lambda b, i: (b, 0, 0)),
                  _const_spec(npre.shape), _const_spec(wqv.shape), _const_spec(wk.shape),
                  _const_spec(wgate.shape), _const_spec(perm.shape), _const_spec(perm.shape)],
        out_specs=[pl.BlockSpec((1, n_q, tm), lambda b, i: (b, 0, i)),
                   pl.BlockSpec((1, 1, n_v, tm), lambda b, i: (b, i, 0, 0)),
                   pl.BlockSpec((1, tm, wk.shape[1]), lambda b, i: (b, i, 0)),
                   pl.BlockSpec((1, tm, wgate.shape[1]), lambda b, i: (b, i, 0))],
        out_shape=[jax.ShapeDtypeStruct((bsz, n_q, s), BF16),
                   jax.ShapeDtypeStruct((bsz, s // tm, n_v, tm), BF16),
                   jax.ShapeDtypeStruct((bsz, s, wk.shape[1]), BF16),
                   jax.ShapeDtypeStruct((bsz, s, wgate.shape[1]), BF16)],
        compiler_params=pltpu.CompilerParams(dimension_semantics=("arbitrary", "arbitrary"),
                                             vmem_limit_bytes=VMEM_LIMIT),
        name="mixer_in_proj",
    )(h, mod, npre, wqv, wk, wgate, perm, perm.T)


def _neg_abs(z):
    bits = lax.bitcast_convert_type(z, jnp.uint32) | jnp.uint32(0x80000000)
    return lax.bitcast_convert_type(bits, F32)


def _sb_kernel(qT_ref, k_ref, vT_ref, o_ref, z_scr, y_scr, w_scr, acc_scr, *, kb, tq):
    qi = pl.program_id(2)
    seg = kb // SUBLANES
    qk_rows = min(SB_QK_ROWS, kb)
    pv_rows = min(MXU_DIM, kb)
    qT = qT_ref[0]
    own = (lax.broadcasted_iota(jnp.int32, qT.shape, 0) // SB_HEAD_DIM) == (pl.program_id(1) % 2)
    qT = jnp.where(own, qT, jnp.zeros_like(qT))
    acc_scr[...] = jnp.zeros_like(acc_scr)

    def scores(j):
        for c0 in range(kb - qk_rows, -1, -qk_rows):
            rows = pl.ds(pl.multiple_of(j * kb + c0, qk_rows), qk_rows)
            z_scr[c0:c0 + qk_rows, :] = _dot(k_ref[0, rows, :], qT)

    def values(j):
        for c0 in range(0, kb, pv_rows):
            acc_scr[...] += _dot(vT_ref[0, j, :, c0:c0 + pv_rows], w_scr[c0:c0 + pv_rows, :])

    def block(j, carry, first):
        if first:
            q_local = lax.broadcasted_iota(jnp.int32, (SUBLANES, tq), 1)
            seg_start = lax.broadcasted_iota(jnp.int32, (SUBLANES, tq), 0) * seg
        else:
            values(j + 1)
        run = jnp.zeros((SUBLANES, tq), F32)
        for r0 in range(kb - SUBLANES, -1, -SUBLANES):
            z = z_scr[r0:r0 + SUBLANES, :]
            sp = jnp.maximum(z, 0.0) + LOG2E * jnp.log(1.0 + jnp.exp2(_neg_abs(z)))
            if first:
                sp = jnp.where(seg_start + r0 // SUBLANES < q_local, sp, 0.0)
            run = run + sp
            y_scr[r0:r0 + SUBLANES, :] = z - run
        s = carry
        offs = [None] * SUBLANES
        for r in range(SUBLANES - 1, -1, -1):
            offs[r] = s
            s = s + run[r:r + 1, :]
        off = jnp.concatenate(offs, axis=0)
        scores(jnp.maximum(j - 1, 0))
        off2 = jnp.concatenate([off, off], axis=0)
        for r0 in range(0, kb, 2 * SUBLANES):
            logw = y_scr[r0:r0 + 2 * SUBLANES, :] - off2
            if first:
                v = r0 // SUBLANES
                before = jnp.concatenate([seg_start + v < q_local, seg_start + (v + 1) < q_local], axis=0)
                logw = jnp.where(before, logw, NEG)
            w_scr[r0:r0 + 2 * SUBLANES, :] = jnp.exp2(logw.astype(BF16))
        return s

    scores(qi)
    carry = block(qi, jnp.zeros((1, tq), F32), True)
    lax.fori_loop(0, qi, lambda i, cr: block(qi - 1 - i, cr, False), carry)
    values(0)
    o_ref[0] = acc_scr[...].astype(o_ref.dtype)


def _sb_call(qT, k, vT, *, kb, tq):
    bsz, _, s = qT.shape
    nkb = s // kb
    dh, pair = SB_HEAD_DIM, 2 * SB_HEAD_DIM
    kern = functools.partial(_sb_kernel, kb=kb, tq=tq)
    return pl.pallas_call(
        kern,
        grid=(bsz, SB_HEADS, s // tq),
        in_specs=[pl.BlockSpec((1, pair, tq), lambda b, h, i: (b, h // 2, i)),
                  pl.BlockSpec((1, s, pair), lambda b, h, i: (b, 0, h // 2)),
                  pl.BlockSpec((1, nkb, dh, kb), lambda b, h, i: (b, 0, h, 0))],
        out_specs=pl.BlockSpec((1, dh, tq), lambda b, h, i: (b, h, i)),
        out_shape=jax.ShapeDtypeStruct((bsz, SB_HEADS * dh, s), BF16),
        scratch_shapes=[pltpu.VMEM((kb, tq), F32), pltpu.VMEM((kb, tq), F32), pltpu.VMEM((kb, tq), BF16),
                        pltpu.VMEM((dh, tq), F32)],
        compiler_params=pltpu.CompilerParams(dimension_semantics=("arbitrary",) * 3,
                                             vmem_limit_bytes=VMEM_LIMIT),
        name="stickbreak_attn",
    )(qT, k, vT)


def _da_kernel(slope_ref, lq1_ref, lk1_ref, lq2_ref, lk2_ref, subln_ref, qT_ref, k_ref, vT_ref, o_ref,
               s_scr, p_scr, base_scr, mx_scr, alpha_scr, m_scr, acc_scr, *, kb, tq, lambda_init):
    h = pl.program_id(1)
    qi = pl.program_id(2)
    slope = slope_ref[h]
    dh = qT_ref.shape[1] // 2
    qk_rows = min(DA_QK_ROWS, kb)
    pv_rows = min(MXU_DIM, kb)
    lane_blocks = tq // LANES
    qT = qT_ref[0]
    qrow = lax.broadcasted_iota(jnp.int32, qT.shape, 0)
    q_pad = (jnp.where(qrow < dh, qT, jnp.zeros_like(qT)), jnp.where(qrow >= dh, qT, jnp.zeros_like(qT)))
    m_scr[...] = jnp.full_like(m_scr, NEG)
    acc_scr[...] = jnp.zeros_like(acc_scr)
    base_scr[...] = slope * lax.broadcasted_iota(jnp.int32, (kb, LANES), 0).astype(F32)

    def scores(j, mp, diagonal):
        mx = jnp.full((SUBLANES, tq), NEG, F32)
        for c0 in range(0, kb, qk_rows):
            rows = pl.ds(pl.multiple_of(j * kb + c0, qk_rows), qk_rows)
            zc = _dot(k_ref[0, rows, :], q_pad[mp])
            if diagonal:
                kl = c0 + lax.broadcasted_iota(jnp.int32, (qk_rows, tq), 0)
                tl = lax.broadcasted_iota(jnp.int32, (qk_rows, tq), 1)
                zc = zc + slope * (tl - jnp.abs(tl - kl)).astype(F32)
                zc = jnp.where((kl >> CHUNK_LOG2) <= (tl >> CHUNK_LOG2), zc, NEG)
            else:
                zc = zc + jnp.concatenate([base_scr[c0:c0 + qk_rows, :]] * lane_blocks, axis=1)
            s_scr[mp, c0:c0 + qk_rows, :] = zc
            for r0 in range(0, qk_rows, SUBLANES):
                mx = jnp.maximum(mx, zc[r0:r0 + SUBLANES, :])
        mx_scr[mp] = mx

    def values(j, mp):
        pv = None
        for c0 in range(0, kb, pv_rows):
            lhs = jnp.concatenate([vT_ref[0, j, :, c0:c0 + pv_rows], jnp.ones((BF16_ROWS, pv_rows), BF16)], axis=0)
            d = _dot(lhs, p_scr[mp, c0:c0 + pv_rows, :])
            pv = d if pv is None else pv + d
        acc_scr[mp] = alpha_scr[mp] * acc_scr[mp] + pv

    def block(j, first):
        cj = 0.0 if first else slope * ((j - qi) * kb).astype(F32)
        if not first:
            for mp in range(2):
                values(j + 1, mp)
        for mp in range(2):
            m_old = m_scr[mp]
            m_new = jnp.maximum(m_old, jnp.max(mx_scr[mp], axis=0, keepdims=True) + cj)
            alpha = jnp.exp2(m_old - m_new)
            shift = jnp.broadcast_to(m_new - cj, (BF16_ROWS, tq))
            for r0 in range(0, kb, BF16_ROWS):
                p_scr[mp, r0:r0 + BF16_ROWS, :] = jnp.exp2(s_scr[mp, r0:r0 + BF16_ROWS, :] - shift).astype(BF16)
            alpha_scr[mp] = alpha
            m_scr[mp] = m_new
            scores(jnp.maximum(j - 1, 0), mp, False)

    for mp in range(2):
        scores(qi, mp, True)
    block(qi, True)

    def body(i, _):
        block(qi - 1 - i, False)
        return 0

    lax.fori_loop(0, qi, body, 0)
    for mp in range(2):
        values(0, mp)

    lam = (jnp.exp(jnp.sum(lq1_ref[...] * lk1_ref[...], axis=1, keepdims=True))
           - jnp.exp(jnp.sum(lq2_ref[...] * lk2_ref[...], axis=1, keepdims=True)) + lambda_init)
    dv = o_ref.shape[1]
    y = (acc_scr[0, 0:dv, :] / acc_scr[0, dv:dv + 1, :]
         - lam * (acc_scr[1, 0:dv, :] / acc_scr[1, dv:dv + 1, :]))
    y = y * lax.rsqrt(jnp.mean(y * y, axis=0, keepdims=True) + EPS)
    o_ref[0] = (y * subln_ref[...] * (1.0 - lambda_init)).astype(o_ref.dtype)


def _da_call(slopes, lq1, lk1, lq2, lk2, subln, qT, k, vT, *, blk0, kb, tq, lambda_init):
    bsz, _, s = qT.shape
    dq2, dv = 2 * DA_HEAD_DIM, DA_V_DIM
    nkb = s // kb
    kern = functools.partial(_da_kernel, kb=kb, tq=tq, lambda_init=lambda_init)
    vec = lambda a: pl.BlockSpec(a.shape, lambda b, h, i: (0, 0))
    return pl.pallas_call(
        kern,
        grid=(bsz, DA_HEADS, s // tq),
        in_specs=[pl.BlockSpec(memory_space=pltpu.SMEM),
                  vec(lq1), vec(lk1), vec(lq2), vec(lk2), vec(subln),
                  pl.BlockSpec((1, dq2, tq), lambda b, h, i: (b, blk0 + h, i)),
                  pl.BlockSpec((1, s, dq2), lambda b, h, i: (b, 0, blk0 + h)),
                  pl.BlockSpec((1, nkb, dv, kb), lambda b, h, i: (b, 0, blk0 + h, 0))],
        out_specs=pl.BlockSpec((1, dv, tq), lambda b, h, i: (b, h, i)),
        out_shape=jax.ShapeDtypeStruct((bsz, DA_HEADS * dv, s), BF16),
        scratch_shapes=[pltpu.VMEM((2, kb, tq), F32), pltpu.VMEM((2, kb, tq), BF16), pltpu.VMEM((kb, LANES), F32),
                        pltpu.VMEM((2, SUBLANES, tq), F32), pltpu.VMEM((2, 1, tq), F32), pltpu.VMEM((2, 1, tq), F32),
                        pltpu.VMEM((2, dv + BF16_ROWS, tq), F32)],
        compiler_params=pltpu.CompilerParams(dimension_semantics=("arbitrary",) * 3,
                                             vmem_limit_bytes=VMEM_LIMIT),
        name="diff_attn",
    )(slopes, lq1, lk1, lq2, lk2, subln, qT, k, vT)


def _merge_ffn_kernel(h_ref, mod_ref, npre_ref, npost_ref, ya_ref, yd_ref, gate_ref, wsb_ref, wda_ref, wout_ref,
                      wg_ref, wu_ref, wd_ref, o_ref, *, fc):
    h = h_ref[0]
    mod = mod_ref[0]
    d = h.shape[1]
    a = lax.dot_general(ya_ref[0], wsb_ref[...], _TN, preferred_element_type=F32)
    b = lax.dot_general(yd_ref[0], wda_ref[...], _TN, preferred_element_type=F32)
    merged = gate_ref[0, :, 0:d].astype(F32) * a + gate_ref[0, :, d:2 * d].astype(F32) * b
    m = _dot(merged.astype(BF16), wout_ref[...])
    h2 = h + mod[5:6, :] * _rms(m, npost_ref[1:2, :])
    o_ref[0] = _swiglu_sublayer(h2, mod, npre_ref[...], npost_ref[...], wg_ref, wu_ref, wd_ref, 2, 0.5, fc)


def _merge_ffn_call(h, mod, npre, npost, ya, yd, gates, wsb, wda, wout, wg, wu, wd, *, tm):
    bsz, s, d = h.shape
    kern = functools.partial(_merge_ffn_kernel, fc=_ffn_chunk(wg.shape[1]))
    tok = lambda w: pl.BlockSpec((1, tm, w), lambda b, i: (b, i, 0))
    return pl.pallas_call(
        kern,
        grid=(bsz, s // tm),
        in_specs=[tok(d), pl.BlockSpec((1, 3 * N_SUB, d), lambda b, i: (b, 0, 0)),
                  _const_spec(npre.shape), _const_spec(npost.shape),
                  pl.BlockSpec((1, ya.shape[1], tm), lambda b, i: (b, 0, i)),
                  pl.BlockSpec((1, yd.shape[1], tm), lambda b, i: (b, 0, i)), tok(gates.shape[2]),
                  _const_spec(wsb.shape), _const_spec(wda.shape), _const_spec(wout.shape),
                  _const_spec(wg.shape), _const_spec(wu.shape), _const_spec(wd.shape)],
        out_specs=tok(d),
        out_shape=jax.ShapeDtypeStruct(h.shape, F32),
        compiler_params=pltpu.CompilerParams(dimension_semantics=("arbitrary", "arbitrary"),
                                             vmem_limit_bytes=VMEM_LIMIT),
        name="merge_out_ffn2",
    )(h, mod, npre, npost, ya, yd, gates, wsb, wda, wout, wg, wu, wd)


def _sb_permutation(kb):
    rho = np.arange(kb)
    p = np.zeros((kb, kb), np.float32)
    p[rho, (rho % SUBLANES) * (kb // SUBLANES) + rho // SUBLANES] = 1.0
    return jnp.asarray(p, dtype=BF16)


def _layer(h, c, w_ada, b_ada, npre, npost, f1g, f1u, f1d, w_in, lq1, lk1, lq2, lk2, subln,
           w_sb, w_da, w_out, f2g, f2u, f2d, lambda_init):
    bsz, s, d = h.shape
    tm = kb = tq = min(512, s)
    sbw = SB_HEADS * SB_HEAD_DIM
    daw = DA_HEADS * 2 * DA_HEAD_DIM
    dvw = DA_HEADS * DA_V_DIM
    assert sbw == daw == dvw, "feature blocks of q / k / v are addressed with one block offset"
    bf = lambda w: w.astype(BF16)
    o = np.cumsum([0, sbw, sbw, sbw, daw, daw, dvw])
    col = lambda i: w_in[:, o[i]:o[i + 1]]
    wqv = bf(jnp.concatenate([col(0), col(3), col(2), col(5)], axis=1).T)
    wk = bf(jnp.concatenate([col(1), col(4)], axis=1))
    wgate = bf(w_in[:, o[6]:])

    mod = _mod_call(c, w_ada, b_ada).reshape(bsz, 3 * N_SUB, d)
    h1 = _ffn_call(h, mod, npre, npost, bf(f1g), bf(f1u), bf(f1d), sub=0, resid_w=0.5, tm=tm)
    qT, vT, k, gates = _proj_call(h1, mod, npre, wqv, wk, wgate, _sb_permutation(kb),
                                  q_scale=LOG2E / math.sqrt(SB_HEAD_DIM), sb_width=sbw, n_q=sbw + daw, tm=tm)
    yaT = _sb_call(qT, k, vT, kb=kb, tq=tq)
    slopes = jnp.asarray(LOG2E * 2.0 ** (-8.0 * (np.arange(DA_HEADS) + 1) / DA_HEADS), dtype=F32)
    row = lambda v: v.reshape(1, -1).astype(F32)
    ydT = _da_call(slopes, row(lq1), row(lk1), row(lq2), row(lk2), subln.reshape(-1, 1).astype(F32),
                   qT, k, vT, blk0=sbw // (2 * DA_HEAD_DIM), kb=kb, tq=tq, lambda_init=lambda_init)
    return _merge_ffn_call(h1, mod, npre, npost, yaT, ydT, gates, bf(w_sb), bf(w_da), bf(w_out),
                           bf(f2g), bf(f2u), bf(f2d), tm=tm)


def kernel(x, c, w_ada, b_ada, norm_pre, norm_post, ffn1_w_gate, ffn1_w_up, ffn1_w_down, w_in, da_lambda_q1, da_lambda_k1, da_lambda_q2, da_lambda_k2, da_subln, w_branch_sb, w_branch_da, w_out, ffn2_w_gate, ffn2_w_up, ffn2_w_down):
    h = x
    for l in range(w_ada.shape[0]):
        lambda_init = 0.8 - 0.6 * math.exp(-0.3 * l)
        h = _layer(h, c, w_ada[l], b_ada[l], norm_pre[l], norm_post[l],
                   ffn1_w_gate[l], ffn1_w_up[l], ffn1_w_down[l], w_in[l],
                   da_lambda_q1[l], da_lambda_k1[l], da_lambda_q2[l], da_lambda_k2[l], da_subln[l],
                   w_branch_sb[l], w_branch_da[l], w_out[l],
                   ffn2_w_gate[l], ffn2_w_up[l], ffn2_w_down[l], lambda_init)
    return h
```
